```python
import math
import jax
import jax.numpy as jnp
from jax import lax
import numpy as np


D_MODEL = 4096
BATCH = 4
SEQ = 4096
DEPTH = 1

MIX_WIDTH = D_MODEL
HEAD_DIM = 128
N_HEADS_DIFF = MIX_WIDTH // 2 // HEAD_DIM
N_HEADS_MOBA = MIX_WIDTH // 2 // HEAD_DIM
DIFF_QK_DIM = HEAD_DIM // 2
W_DIFF_QK = N_HEADS_DIFF * 2 * DIFF_QK_DIM
W_DIFF_V = N_HEADS_DIFF * HEAD_DIM
W_MOBA = N_HEADS_MOBA * HEAD_DIM
IN_COLS = 2 * W_DIFF_QK + W_DIFF_V + 3 * W_MOBA
ATTN_Q_BLOCK = 128
MOBA_BLOCK = 256
MOBA_TOPK = 3
MOBA_Q_CHUNK = 16
N_EXPERTS = 32
TOP_K = 4
D_FF = (D_MODEL * 3) // 8
SWIGLU_LIMIT = 7.0
SWIGLU_ALPHA = 1.702
NORM_EPS = 1e-5
ALIBI_MAX_BIAS = 8.0

kernel_name = 'hybrid_diffattn_moba_moe_block'


def _rmsnorm(x, w):
    xf = x.astype(jnp.float32)
    y = xf * lax.rsqrt(jnp.mean(xf * xf, axis=-1, keepdims=True) + NORM_EPS)
    return (y * w.astype(jnp.float32)).astype(x.dtype)


def _alibi_slopes(n):
    return jnp.exp2(-ALIBI_MAX_BIAS * jnp.arange(1, n + 1, dtype=jnp.float32) / n)


def _diff_attention(q, k, v, lam_q1, lam_k1, lam_q2, lam_k2, subln_w, lambda_init):
    B, S, H, _, DQ = q.shape
    f32 = jnp.float32
    qh = q.transpose(0, 2, 3, 1, 4)
    kh = k.transpose(0, 2, 3, 1, 4)
    vh = v.transpose(0, 2, 1, 3)
    lam = (jnp.exp(jnp.sum(lam_q1.astype(f32) * lam_k1.astype(f32)))
           - jnp.exp(jnp.sum(lam_q2.astype(f32) * lam_k2.astype(f32))) + lambda_init)
    slopes = _alibi_slopes(H)
    scale = DQ ** -0.5
    kpos = jnp.arange(S)

    def block(i):
        t0 = i * ATTN_Q_BLOCK
        qb = lax.dynamic_slice_in_dim(qh, t0, ATTN_Q_BLOCK, axis=3)
        s = jnp.einsum('bhcqd,bhckd->bhcqk', qb, kh).astype(f32) * scale
        dist = (t0 + jnp.arange(ATTN_Q_BLOCK))[:, None] - kpos[None, :]
        bias = -slopes[:, None, None] * dist.astype(f32)
        s = jnp.where((dist >= 0)[None, None, None], s + bias[None, :, None], -jnp.inf)
        p = jax.nn.softmax(s, axis=-1)
        a = p[:, :, 0] - lam * p[:, :, 1]
        return jnp.einsum('bhqk,bhkd->bqhd', a.astype(vh.dtype), vh)

    o = lax.map(block, jnp.arange(S // ATTN_Q_BLOCK))
    o = o.transpose(1, 0, 2, 3, 4).reshape(B, S, H, -1)
    o = _rmsnorm(o, subln_w) * (1.0 - lambda_init)
    return o.reshape(B, S, -1)


def _moba_attention(q, k, v):
    B, S, H, D = q.shape
    f32 = jnp.float32
    nb = -(-S // MOBA_BLOCK)
    sp = nb * MOBA_BLOCK
    topk = min(MOBA_TOPK, nb)
    scale = D ** -0.5
    qh = q.transpose(0, 2, 1, 3)
    pad = ((0, 0), (0, 0), (0, sp - S), (0, 0))
    kh = jnp.pad(k.transpose(0, 2, 1, 3), pad)
    vh = jnp.pad(v.transpose(0, 2, 1, 3), pad)
    kb = kh.reshape(B, H, nb, MOBA_BLOCK, D)
    vb = vh.reshape(B, H, nb, MOBA_BLOCK, D)
    kmean = jnp.mean(kb.astype(f32), axis=3)
    gate = jnp.einsum('bhsd,bhnd->bhsn', qh.astype(f32), kmean)
    past = jnp.arange(nb)[None, :] < (jnp.arange(S) // MOBA_BLOCK)[:, None]
    gate = jnp.where(past, gate, -jnp.inf)
    top_val, top_idx = lax.top_k(gate, topk)
    top_ok = top_val > -jnp.inf
    slopes = _alibi_slopes(H)
    bi = jnp.arange(B)[:, None, None, None]
    hi = jnp.arange(H)[None, :, None, None]
    boff = jnp.arange(MOBA_BLOCK)

    def chunk(i):
        t0 = i * MOBA_Q_CHUNK
        qc = lax.dynamic_slice_in_dim(qh, t0, MOBA_Q_CHUNK, axis=2)
        idx = lax.dynamic_slice_in_dim(top_idx, t0, MOBA_Q_CHUNK, axis=2)
        ok = lax.dynamic_slice_in_dim(top_ok, t0, MOBA_Q_CHUNK, axis=2)
        ksel = kb[bi, hi, idx]
        vsel = vb[bi, hi, idx]
        tpos = t0 + jnp.arange(MOBA_Q_CHUNK)
        s_sel = jnp.einsum('bhqd,bhqknd->bhqkn', qc, ksel).astype(f32) * scale
        dist_sel = tpos[None, None, :, None, None] - (idx[..., None] * MOBA_BLOCK + boff)
        s_sel = jnp.where(ok[..., None],
                          s_sel - slopes[None, :, None, None, None] * dist_sel.astype(f32),
                          -jnp.inf)
        ob = t0 // MOBA_BLOCK
        kown = lax.dynamic_slice_in_dim(kh, ob * MOBA_BLOCK, MOBA_BLOCK, axis=2)
        vown = lax.dynamic_slice_in_dim(vh, ob * MOBA_BLOCK, MOBA_BLOCK, axis=2)
        s_own = jnp.einsum('bhqd,bhnd->bhqn', qc, kown).astype(f32) * scale
        dist_own = tpos[:, None] - (ob * MOBA_BLOCK + boff)[None, :]
        s_own = jnp.where(dist_own >= 0,
                          s_own - slopes[None, :, None, None] * dist_own.astype(f32),
                          -jnp.inf)
        s_all = jnp.concatenate([s_sel.reshape(B, H, MOBA_Q_CHUNK, topk * MOBA_BLOCK), s_own], axis=-1)
        p = jax.nn.softmax(s_all, axis=-1).astype(v.dtype)
        p_sel = p[..., :topk * MOBA_BLOCK].reshape(B, H, MOBA_Q_CHUNK, topk, MOBA_BLOCK)
        p_own = p[..., topk * MOBA_BLOCK:]
        return (jnp.einsum('bhqkn,bhqknd->bqhd', p_sel, vsel)
                + jnp.einsum('bhqn,bhnd->bqhd', p_own, vown))

    o = lax.map(chunk, jnp.arange(S // MOBA_Q_CHUNK))
    return o.transpose(1, 0, 2, 3, 4).reshape(B, S, H * D)


def _moe(x, w_router, b_router, w_gate_up, b_gate_up, w_down, b_down):
    B, S, D = x.shape
    f32 = jnp.float32
    xf = x.reshape(-1, D)
    logits = (xf @ w_router + b_router).astype(f32)
    top_val, top_idx = lax.top_k(logits, TOP_K)
    gates = jax.nn.softmax(top_val, axis=-1)
    combine = jnp.einsum('tk,tke->te', gates, jax.nn.one_hot(top_idx, N_EXPERTS, dtype=f32))
    y = jnp.zeros((xf.shape[0], D), f32)
    for e in range(N_EXPERTS):
        hgu = xf @ w_gate_up[e] + b_gate_up[e]
        g = jnp.minimum(hgu[:, 0::2], SWIGLU_LIMIT)
        u = jnp.clip(hgu[:, 1::2], -SWIGLU_LIMIT, SWIGLU_LIMIT)
        act = (u + 1.0) * (g * jax.nn.sigmoid(g * SWIGLU_ALPHA))
        out = act @ w_down[e] + b_down[e]
        y = y + combine[:, e:e + 1] * out.astype(f32)
    return y.astype(x.dtype).reshape(B, S, D)


def setup_inputs(seed: int = 0) -> dict:
    key = jax.random.key(seed)
    ks = jax.random.split(key, 17)
    f32 = jnp.float32
    nrm = lambda k, shape, s: jax.random.normal(k, shape, f32) * s
    return {
        'x': nrm(ks[0], (BATCH, SEQ, D_MODEL), 1.0),
        'norm1_w': 1.0 + nrm(ks[1], (DEPTH, D_MODEL), 0.01),
        'w_in': nrm(ks[2], (DEPTH, D_MODEL, IN_COLS), D_MODEL ** -0.5),
        'lam_q1': nrm(ks[3], (DEPTH, DIFF_QK_DIM), 0.1),
        'lam_k1': nrm(ks[4], (DEPTH, DIFF_QK_DIM), 0.1),
        'lam_q2': nrm(ks[5], (DEPTH, DIFF_QK_DIM), 0.1),
        'lam_k2': nrm(ks[6], (DEPTH, DIFF_QK_DIM), 0.1),
        'subln_w': 1.0 + nrm(ks[7], (DEPTH, HEAD_DIM), 0.01),
        'w_out': nrm(ks[8], (DEPTH, MIX_WIDTH, D_MODEL), MIX_WIDTH ** -0.5),
        'norm2_w': 1.0 + nrm(ks[9], (DEPTH, D_MODEL), 0.01),
        'w_router': nrm(ks[10], (DEPTH, D_MODEL, N_EXPERTS), D_MODEL ** -0.5),
        'b_router': nrm(ks[11], (DEPTH, N_EXPERTS), 0.01),
        'w_gate_up': nrm(ks[12], (DEPTH, N_EXPERTS, D_MODEL, 2 * D_FF), D_MODEL ** -0.5),
        'b_gate_up': nrm(ks[13], (DEPTH, N_EXPERTS, 2 * D_FF), 0.01),
        'w_down': nrm(ks[14], (DEPTH, N_EXPERTS, D_FF, D_MODEL), D_FF ** -0.5),
        'b_down': nrm(ks[15], (DEPTH, N_EXPERTS, D_MODEL), 0.01),
        'final_norm_w': 1.0 + nrm(ks[16], (D_MODEL,), 0.01),
    }


def reference(x, norm1_w, w_in, lam_q1, lam_k1, lam_q2, lam_k2, subln_w, w_out,
              norm2_w, w_router, b_router, w_gate_up, b_gate_up, w_down, b_down,
              final_norm_w):
    B, S, _ = x.shape
    o1 = W_DIFF_QK
    o2 = o1 + W_DIFF_QK
    o3 = o2 + W_DIFF_V
    o4 = o3 + W_MOBA
    o5 = o4 + W_MOBA
    h = x
    for l in range(DEPTH):
        lambda_init = 0.8 - 0.6 * math.exp(-0.3 * l)
        n1 = _rmsnorm(h, norm1_w[l])
        proj = n1 @ w_in[l]
        q_a = proj[..., :o1].reshape(B, S, N_HEADS_DIFF, 2, DIFF_QK_DIM)
        k_a = proj[..., o1:o2].reshape(B, S, N_HEADS_DIFF, 2, DIFF_QK_DIM)
        v_a = proj[..., o2:o3].reshape(B, S, N_HEADS_DIFF, HEAD_DIM)
        q_b = proj[..., o3:o4].reshape(B, S, N_HEADS_MOBA, HEAD_DIM)
        k_b = proj[..., o4:o5].reshape(B, S, N_HEADS_MOBA, HEAD_DIM)
        v_b = proj[..., o5:].reshape(B, S, N_HEADS_MOBA, HEAD_DIM)
        y_a = _diff_attention(q_a, k_a, v_a, lam_q1[l], lam_k1[l], lam_q2[l], lam_k2[l],
                              subln_w[l], lambda_init)
        y_b = _moba_attention(q_b, k_b, v_b)
        mix = jnp.concatenate([y_a, y_b], axis=-1)
        h = h + mix @ w_out[l]
        n2 = _rmsnorm(h, norm2_w[l])
        h = h + _moe(n2, w_router[l], b_router[l], w_gate_up[l], b_gate_up[l],
                     w_down[l], b_down[l])
    return _rmsnorm(h, final_norm_w)
```

```python
import functools
import math

import jax
import jax.numpy as jnp
from jax import lax
from jax.experimental import pallas as pl
from jax.experimental.pallas import tpu as pltpu

F32 = jnp.float32
BF16 = jnp.bfloat16

HEAD_DIM = 128
DIFF_QK_DIM = HEAD_DIM // 2
MOBA_BLOCK = 256
MOBA_TOPK = 3
TOP_K = 4
SWIGLU_LIMIT = 7.0
SWIGLU_ALPHA = 1.702
NORM_EPS = 1e-5
ALIBI_MAX_BIAS = 8.0

V7X_VMEM_BYTES = 64 * 1024 * 1024
VMEM_LIMIT_BYTES = V7X_VMEM_BYTES - 8 * 1024 * 1024
LANES = 128


def _params(*semantics):
    return pltpu.CompilerParams(dimension_semantics=semantics,
                                vmem_limit_bytes=VMEM_LIMIT_BYTES)


def _rmsnorm_kernel(x_ref, w_ref, o_ref):
    x = x_ref[...]
    ms = jnp.mean(x * x, axis=-1, keepdims=True)
    o_ref[...] = ((x * lax.rsqrt(ms + NORM_EPS)) * w_ref[...]).astype(o_ref.dtype)


def _rmsnorm(x, w, tm):
    t, d = x.shape
    return pl.pallas_call(
        _rmsnorm_kernel,
        out_shape=jax.ShapeDtypeStruct((t, d), BF16),
        grid=(t // tm,),
        in_specs=[pl.BlockSpec((tm, d), lambda i: (i, 0)),
                  pl.BlockSpec((1, d), lambda i: (0, 0))],
        out_specs=pl.BlockSpec((tm, d), lambda i: (i, 0)),
        compiler_params=_params("parallel"),
    )(x, w.reshape(1, d))


def _matmul_kernel(a_ref, b_ref, o_ref):
    o_ref[...] = jnp.dot(a_ref[...], b_ref[...],
                         preferred_element_type=F32).astype(o_ref.dtype)


def _matmul(a, b, tm, tn, out_dtype):
    m, k = a.shape
    _, n = b.shape
    return pl.pallas_call(
        _matmul_kernel,
        out_shape=jax.ShapeDtypeStruct((m, n), out_dtype),
        grid=(m // tm, n // tn),
        in_specs=[pl.BlockSpec((tm, k), lambda i, j: (i, 0)),
                  pl.BlockSpec((k, tn), lambda i, j: (0, j))],
        out_specs=pl.BlockSpec((tm, tn), lambda i, j: (i, j)),
        compiler_params=_params("parallel", "parallel"),
    )(a, b)


def _outproj_kernel(ya_ref, yb_ref, w_ref, x_ref, o_ref):
    ka = ya_ref.shape[1]
    acc = jnp.dot(ya_ref[...], w_ref[:ka, :], preferred_element_type=F32)
    acc += jnp.dot(yb_ref[...], w_ref[ka:, :], preferred_element_type=F32)
    o_ref[...] = x_ref[...] + acc


def _outproj(ya, yb, w, x, tm, tn):
    m, ka = ya.shape
    kb = yb.shape[1]
    n = w.shape[1]
    return pl.pallas_call(
        _outproj_kernel,
        out_shape=jax.ShapeDtypeStruct((m, n), F32),
        grid=(m // tm, n // tn),
        in_specs=[pl.BlockSpec((tm, ka), lambda i, j: (i, 0)),
                  pl.BlockSpec((tm, kb), lambda i, j: (i, 0)),
                  pl.BlockSpec((ka + kb, tn), lambda i, j: (0, j)),
                  pl.BlockSpec((tm, tn), lambda i, j: (i, j))],
        out_specs=pl.BlockSpec((tm, tn), lambda i, j: (i, j)),
        compiler_params=_params("parallel", "parallel"),
    )(ya, yb, w, x)


def _qk(q, k):
    return lax.dot_general(q, k, (((1,), (1,)), ((), ())), preferred_element_type=F32)


def _online_step(s, shift, v, m, l, acc):
    m_new = jnp.maximum(m, jnp.max(s, axis=-1, keepdims=True) - shift)
    p = jnp.exp(s - (m_new + shift))
    alpha = jnp.exp(m - m_new)
    l_new = alpha * l + jnp.sum(p, axis=-1, keepdims=True)
    acc_new = alpha * acc + jnp.dot(p.astype(v.dtype), v, preferred_element_type=F32)
    return m_new, l_new, acc_new


def _local_alibi(slope, tq, tk):
    row = lax.broadcasted_iota(jnp.int32, (tq, tk), 0)
    col = lax.broadcasted_iota(jnp.int32, (tq, tk), 1)
    delta = row - col
    return delta, (-slope) * delta.astype(F32)


def _diff_attn_kernel(slopes_ref, lam_ref, subw_ref, q_ref, k_ref, v_ref, o_ref, *,
                      tq, lambda_init):
    h = pl.program_id(1)
    i = pl.program_id(2)
    slope = slopes_ref[h]
    dq = DIFF_QK_DIM

    q = q_ref[...] * jnp.asarray(dq ** -0.5, q_ref.dtype)
    lane = lax.broadcasted_iota(jnp.int32, q.shape, 1)
    zero = jnp.zeros_like(q)
    q1 = jnp.where(lane < dq, q, zero)
    q2 = jnp.where(lane >= dq, q, zero)

    delta, bias0 = _local_alibi(slope, tq, tq)

    def tile(j, shift, carry, masked):
        start = pl.multiple_of(j * tq, tq)
        k = k_ref[pl.ds(start, tq), :]
        v = v_ref[pl.ds(start, tq), :]
        s1 = _qk(q1, k) + bias0
        s2 = _qk(q2, k) + bias0
        if masked:
            neg = jnp.full_like(s1, -jnp.inf)
            s1 = jnp.where(delta >= 0, s1, neg)
            s2 = jnp.where(delta >= 0, s2, neg)
        m1, l1, a1, m2, l2, a2 = carry
        m1, l1, a1 = _online_step(s1, shift, v, m1, l1, a1)
        m2, l2, a2 = _online_step(s2, shift, v, m2, l2, a2)
        return m1, l1, a1, m2, l2, a2

    neg_col = jnp.full((tq, 1), -jnp.inf, F32)
    zero_col = jnp.zeros((tq, 1), F32)
    zero_acc = jnp.zeros((tq, HEAD_DIM), F32)
    carry = (neg_col, zero_col, zero_acc, neg_col, zero_col, zero_acc)
    carry = tile(i, jnp.float32(0.0), carry, masked=True)

    def body(j, c):
        shift = slope * ((i - j) * tq).astype(F32)
        return tile(j, shift, c, masked=False)

    m1, l1, a1, m2, l2, a2 = lax.fori_loop(0, i, body, carry)

    lam_v = lam_ref[...]
    lam = (jnp.exp(jnp.sum(lam_v[0:1, :] * lam_v[1:2, :], axis=-1, keepdims=True))
           - jnp.exp(jnp.sum(lam_v[2:3, :] * lam_v[3:4, :], axis=-1, keepdims=True))
           + lambda_init)
    o = a1 / l1 - lam * (a2 / l2)
    ms = jnp.mean(o * o, axis=-1, keepdims=True)
    o = (o * lax.rsqrt(ms + NORM_EPS)) * subw_ref[...]
    o_ref[...] = (o * (1.0 - lambda_init)).astype(o_ref.dtype)


def _diff_attention(proj3, slopes, lam4, subw, n_heads, q_blk, k_blk, v_blk, tq, lambda_init):
    b, s, _ = proj3.shape
    kern = functools.partial(_diff_attn_kernel, tq=tq, lambda_init=lambda_init)
    return pl.pallas_call(
        kern,
        out_shape=jax.ShapeDtypeStruct((b, s, n_heads * HEAD_DIM), BF16),
        grid=(b, n_heads, s // tq),
        in_specs=[pl.BlockSpec(memory_space=pltpu.SMEM),
                  pl.BlockSpec((4, DIFF_QK_DIM), lambda bi, h, i: (0, 0)),
                  pl.BlockSpec((1, HEAD_DIM), lambda bi, h, i: (0, 0)),
                  pl.BlockSpec((None, tq, HEAD_DIM), lambda bi, h, i: (bi, i, q_blk + h)),
                  pl.BlockSpec((None, s, HEAD_DIM), lambda bi, h, i: (bi, 0, k_blk + h)),
                  pl.BlockSpec((None, s, HEAD_DIM), lambda bi, h, i: (bi, 0, v_blk + h))],
        out_specs=pl.BlockSpec((None, tq, HEAD_DIM), lambda bi, h, i: (bi, i, h)),
        compiler_params=_params("parallel", "parallel", "arbitrary"),
    )(slopes, lam4, subw, proj3, proj3, proj3)


def _moba_kernel(slopes_ref, q_ref, k_ref, v_ref, o_ref, kmean_ref, *, n_blocks):
    h = pl.program_id(1)
    i = pl.program_id(2)
    slope = slopes_ref[h]
    blk = MOBA_BLOCK
    scale = HEAD_DIM ** -0.5

    @pl.when(i == 0)
    def _():
        for n in range(n_blocks):
            kb = k_ref[n * blk:(n + 1) * blk, :].astype(F32)
            kmean_ref[n:n + 1, :] = jnp.mean(kb, axis=0, keepdims=True)

    q = q_ref[...]
    km = kmean_ref[...]
    km_hi = km.astype(BF16)
    km_lo = (km - km_hi.astype(F32)).astype(BF16)
    gate = _qk(q, km_hi) + _qk(q, km_lo)
    blk_id = lax.broadcasted_iota(jnp.int32, gate.shape, 1)
    blk_idf = blk_id.astype(F32)
    neg_gate = jnp.full_like(gate, -jnp.inf)
    g = jnp.where(blk_id < i, gate, neg_gate)
    sel = jnp.zeros_like(gate)
    for _ in range(min(MOBA_TOPK, n_blocks)):
        mx = jnp.max(g, axis=-1, keepdims=True)
        first = jnp.min(jnp.where(g == mx, blk_idf, float(n_blocks)), axis=-1, keepdims=True)
        pick = (blk_idf == first) & (mx > -jnp.inf)
        sel = jnp.where(pick, 1.0, sel)
        g = jnp.where(pick, neg_gate, g)

    delta, bias0 = _local_alibi(slope, blk, blk)
    neg = jnp.full((blk, blk), -jnp.inf, F32)

    def tile(j, shift, carry, keep):
        start = pl.multiple_of(j * blk, blk)
        k = k_ref[pl.ds(start, blk), :]
        v = v_ref[pl.ds(start, blk), :]
        s = jnp.where(keep, _qk(q, k) * scale + bias0, neg)
        return _online_step(s, shift, v, *carry)

    carry = (jnp.full((blk, 1), -jnp.inf, F32), jnp.zeros((blk, 1), F32),
             jnp.zeros((blk, HEAD_DIM), F32))
    carry = tile(i, jnp.float32(0.0), carry, delta >= 0)

    def body(j, c):
        chosen = jnp.sum(jnp.where(blk_id == j, sel, 0.0), axis=-1, keepdims=True)
        shift = slope * ((i - j) * blk).astype(F32)
        return tile(j, shift, c, chosen > 0.0)

    m, l, acc = lax.fori_loop(0, i, body, carry)
    o_ref[...] = (acc / l).astype(o_ref.dtype)


def _moba_attention(proj3, slopes, n_heads, q_blk, k_blk, v_blk):
    b, s, _ = proj3.shape
    n_blocks = s // MOBA_BLOCK
    kern = functools.partial(_moba_kernel, n_blocks=n_blocks)
    return pl.pallas_call(
        kern,
        out_shape=jax.ShapeDtypeStruct((b, s, n_heads * HEAD_DIM), BF16),
        grid=(b, n_heads, n_blocks),
        in_specs=[pl.BlockSpec(memory_space=pltpu.SMEM),
                  pl.BlockSpec((None, MOBA_BLOCK, HEAD_DIM), lambda bi, h, i: (bi, i, q_blk + h)),
                  pl.BlockSpec((None, s, HEAD_DIM), lambda bi, h, i: (bi, 0, k_blk + h)),
                  pl.BlockSpec((None, s, HEAD_DIM), lambda bi, h, i: (bi, 0, v_blk + h))],
        out_specs=pl.BlockSpec((None, MOBA_BLOCK, HEAD_DIM), lambda bi, h, i: (bi, i, h)),
        scratch_shapes=[pltpu.VMEM((n_blocks, HEAD_DIM), F32)],
        compiler_params=_params("parallel", "parallel", "arbitrary"),
    )(slopes, proj3, proj3, proj3)


def _pack_cols(a, b):
    ua = lax.bitcast_convert_type(a, jnp.uint32)
    ub = lax.bitcast_convert_type(b, jnp.uint32)
    return (ua >> 16) | (ub & jnp.uint32(0xFFFF0000))


def _unpack_cols(w):
    lo = lax.bitcast_convert_type(w << 16, F32)
    hi = lax.bitcast_convert_type(w & jnp.uint32(0xFFFF0000), F32)
    return lo.astype(BF16), hi.astype(BF16)


def _lanes_from_columns(cols, dtype):
    tm = cols[0].shape[0]
    lane = lax.broadcasted_iota(jnp.int32, (tm, len(cols)), 1)
    out = jnp.zeros((tm, len(cols)), dtype)
    for r, c in enumerate(cols):
        out = jnp.where(lane == r, c.astype(dtype), out)
    return out


def _router_kernel(h_ref, w_ref, wr_ref, br_ref, xp_ref, ids_ref, gates_ref):
    x = h_ref[...]
    ms = jnp.mean(x * x, axis=-1, keepdims=True)
    n2 = (x * lax.rsqrt(ms + NORM_EPS)) * w_ref[...]
    hi = n2.astype(BF16)
    hi_f = hi.astype(F32)
    half = n2.shape[1] // 2
    xp_ref[...] = _pack_cols(hi_f[:, :half], hi_f[:, half:])

    lo = (n2 - hi_f).astype(BF16)
    wr = wr_ref[...]
    wr_hi = wr.astype(BF16)
    wr_lo = (wr - wr_hi.astype(F32)).astype(BF16)
    logits = (jnp.dot(hi, wr_hi, preferred_element_type=F32)
              + jnp.dot(lo, wr_hi, preferred_element_type=F32)
              + jnp.dot(hi, wr_lo, preferred_element_type=F32)) + br_ref[...]

    n_exp = logits.shape[1]
    eid = lax.broadcasted_iota(jnp.int32, logits.shape, 1).astype(F32)
    neg = jnp.full_like(logits, -jnp.inf)
    g = logits
    vals, idxs = [], []
    for _ in range(TOP_K):
        mx = jnp.max(g, axis=-1, keepdims=True)
        first = jnp.min(jnp.where(g == mx, eid, float(n_exp)), axis=-1, keepdims=True)
        vals.append(mx)
        idxs.append(first)
        g = jnp.where(eid == first, neg, g)
    exps = [jnp.exp(v - vals[0]) for v in vals]
    denom = exps[0] + exps[1] + exps[2] + exps[3]
    ids_ref[...] = _lanes_from_columns(idxs, jnp.int32)
    gates_ref[...] = _lanes_from_columns([e / denom for e in exps], F32)


def _router(h, norm_w, w_router, b_router, tm):
    t, d = h.shape
    n_exp = w_router.shape[1]
    return pl.pallas_call(
        _router_kernel,
        out_shape=(jax.ShapeDtypeStruct((t, d // 2), jnp.uint32),
                   jax.ShapeDtypeStruct((t, TOP_K), jnp.int32),
                   jax.ShapeDtypeStruct((t, TOP_K), F32)),
        grid=(t // tm,),
        in_specs=[pl.BlockSpec((tm, d), lambda i: (i, 0)),
                  pl.BlockSpec((1, d), lambda i: (0, 0)),
                  pl.BlockSpec((d, n_exp), lambda i: (0, 0)),
                  pl.BlockSpec((1, n_exp), lambda i: (0, 0))],
        out_specs=(pl.BlockSpec((tm, d // 2), lambda i: (i, 0)),
                   pl.BlockSpec((tm, TOP_K), lambda i: (i, 0)),
                   pl.BlockSpec((tm, TOP_K), lambda i: (i, 0))),
        compiler_params=_params("parallel"),
    )(h, norm_w.reshape(1, d), w_router, b_router.reshape(1, n_exp))


def _rank_kernel(ids_ref, rank_ref, counts_ref, carry_ref, *, n_exp):
    i = pl.program_id(0)

    @pl.when(i == 0)
    def _():
        carry_ref[...] = jnp.zeros_like(carry_ref)

    ids = ids_ref[...]
    tm = ids.shape[0]
    eid = lax.broadcasted_iota(jnp.int32, (tm, n_exp), 1)
    onehots = [(eid == ids[:, r:r + 1]).astype(F32) for r in range(TOP_K)]
    total = onehots[0] + onehots[1] + onehots[2] + onehots[3]
    row = lax.broadcasted_iota(jnp.int32, (tm, tm), 0)
    col = lax.broadcasted_iota(jnp.int32, (tm, tm), 1)
    strict_lower = (col < row).astype(BF16)
    before = jnp.dot(strict_lower, total.astype(BF16), preferred_element_type=F32)
    before = before + carry_ref[...]
    ranks = [jnp.sum(oh * before, axis=-1, keepdims=True) for oh in onehots]
    rank_ref[...] = _lanes_from_columns(ranks, jnp.int32)
    carry_ref[...] += jnp.sum(total, axis=0, keepdims=True)
    counts_ref[...] = carry_ref[...]


def _rank(ids, n_exp, tm):
    t = ids.shape[0]
    return pl.pallas_call(
        functools.partial(_rank_kernel, n_exp=n_exp),
        out_shape=(jax.ShapeDtypeStruct((t, TOP_K), jnp.int32),
                   jax.ShapeDtypeStruct((1, n_exp), F32)),
        grid=(t // tm,),
        in_specs=[pl.BlockSpec((tm, TOP_K), lambda i: (i, 0))],
        out_specs=(pl.BlockSpec((tm, TOP_K), lambda i: (i, 0)),
                   pl.BlockSpec((1, n_exp), lambda i: (0, 0))),
        scratch_shapes=[pltpu.VMEM((1, n_exp), F32)],
        compiler_params=_params("arbitrary"),
    )(ids)


def _dispatch_kernel(pos_ref, x_ref, buf_in_ref, buf_ref, sem):
    del buf_in_ref
    n = pos_ref.shape[0]

    def row_copy(r):
        return pltpu.make_async_copy(x_ref.at[pl.ds(r // TOP_K, 1), :],
                                     buf_ref.at[pl.ds(pos_ref[r], 1), :], sem)

    def start(r, c):
        row_copy(r).start()
        return c

    def wait(r, c):
        row_copy(r).wait()
        return c

    lax.fori_loop(0, n, start, 0)
    lax.fori_loop(0, n, wait, 0)


def _dispatch(pos_flat, xp, buf, tm):
    t, w = xp.shape
    return pl.pallas_call(
        _dispatch_kernel,
        out_shape=jax.ShapeDtypeStruct(buf.shape, buf.dtype),
        grid=(t // tm,),
        in_specs=[pl.BlockSpec((tm * TOP_K,), lambda i: (i,), memory_space=pltpu.SMEM),
                  pl.BlockSpec((tm, w), lambda i: (i, 0)),
                  pl.BlockSpec(memory_space=pl.ANY)],
        out_specs=pl.BlockSpec(memory_space=pl.ANY),
        scratch_shapes=[pltpu.SemaphoreType.DMA(())],
        input_output_aliases={2: 0},
        compiler_params=_params("arbitrary"),
    )(pos_flat, xp, buf)


def _gateup_kernel(te_ref, tv_ref, x_ref, wg_ref, wu_ref, bg_ref, bu_ref, o_ref):
    i = pl.program_id(1)

    @pl.when(tv_ref[i] == 1)
    def _():
        xa, xb = _unpack_cols(x_ref[...])
        half = xa.shape[1]

        def proj(w_ref, b_ref):
            return (jnp.dot(xa, w_ref[:half, :], preferred_element_type=F32)
                    + jnp.dot(xb, w_ref[half:, :], preferred_element_type=F32)
                    + b_ref[...])

        g = jnp.minimum(proj(wg_ref, bg_ref), SWIGLU_LIMIT)
        u = jnp.clip(proj(wu_ref, bu_ref), -SWIGLU_LIMIT, SWIGLU_LIMIT)
        sig = 1.0 / (1.0 + jnp.exp(-(g * SWIGLU_ALPHA)))
        o_ref[...] = ((u + 1.0) * (g * sig)).astype(o_ref.dtype)

    @pl.when(tv_ref[i] == 0)
    def _():
        o_ref[...] = jnp.zeros_like(o_ref)


def _gateup(tile_expert, tile_valid, xs, wg, wu, bg, bu, tm, fc):
    rows, half = xs.shape
    n_exp, d, f = wg.shape
    n_tiles = rows // tm
    n_chunks = f // fc
    last_chunk = n_chunks - 1

    def w_map(c, i, te, tv):
        return (te[i], 0, jnp.where(tv[i] == 1, c, last_chunk))

    grid_spec = pltpu.PrefetchScalarGridSpec(
        num_scalar_prefetch=2,
        grid=(n_chunks, n_tiles),
        in_specs=[pl.BlockSpec((tm, half), lambda c, i, te, tv: (i, 0)),
                  pl.BlockSpec((None, d, fc), w_map),
                  pl.BlockSpec((None, d, fc), w_map),
                  pl.BlockSpec((None, 1, fc), w_map),
                  pl.BlockSpec((None, 1, fc), w_map)],
        out_specs=pl.BlockSpec((tm, fc), lambda c, i, te, tv: (i, c)),
    )
    return pl.pallas_call(
        _gateup_kernel,
        out_shape=jax.ShapeDtypeStruct((rows, f), BF16),
        grid_spec=grid_spec,
        compiler_params=_params("arbitrary", "arbitrary"),
    )(tile_expert, tile_valid, xs, wg, wu, bg, bu)


def _down_kernel(te_ref, tv_ref, a_ref, w_ref, b_ref, o_ref):
    i = pl.program_id(0)

    @pl.when(tv_ref[i] == 1)
    def _():
        o_ref[...] = jnp.dot(a_ref[...], w_ref[...], preferred_element_type=F32) + b_ref[...]

    @pl.when(tv_ref[i] == 0)
    def _():
        o_ref[...] = jnp.zeros_like(o_ref)


def _down(tile_expert, tile_valid, act, wd, bd, tm):
    rows, f = act.shape
    n_exp, _, d = wd.shape
    grid_spec = pltpu.PrefetchScalarGridSpec(
        num_scalar_prefetch=2,
        grid=(rows // tm,),
        in_specs=[pl.BlockSpec((tm, f), lambda i, te, tv: (i, 0)),
                  pl.BlockSpec((None, f, d), lambda i, te, tv: (te[i], 0, 0)),
                  pl.BlockSpec((None, 1, d), lambda i, te, tv: (te[i], 0, 0))],
        out_specs=pl.BlockSpec((tm, d), lambda i, te, tv: (i, 0)),
    )
    return pl.pallas_call(
        _down_kernel,
        out_shape=jax.ShapeDtypeStruct((rows, d), F32),
        grid_spec=grid_spec,
        compiler_params=_params("arbitrary"),
    )(tile_expert, tile_valid, act, wd, bd)


def _combine_kernel(pos_ref, gates_ref, h_ref, w_ref, y_ref, o_ref, rows_ref, sem):
    tm = h_ref.shape[0]
    n = tm * TOP_K

    def row_copy(r):
        return pltpu.make_async_copy(y_ref.at[pl.ds(pos_ref[r], 1), :],
                                     rows_ref.at[r % TOP_K, pl.ds(r // TOP_K, 1), :], sem)

    def start(r, c):
        row_copy(r).start()
        return c

    def wait(r, c):
        row_copy(r).wait()
        return c

    lax.fori_loop(0, n, start, 0)
    lax.fori_loop(0, n, wait, 0)

    gates = gates_ref[...]
    moe = gates[:, 0:1] * rows_ref[0]
    for r in range(1, TOP_K):
        moe += gates[:, r:r + 1] * rows_ref[r]
    x = h_ref[...] + moe
    ms = jnp.mean(x * x, axis=-1, keepdims=True)
    o_ref[...] = (x * lax.rsqrt(ms + NORM_EPS)) * w_ref[...]


def _combine(pos_flat, gates, h, norm_w, ys, tm):
    t, d = h.shape
    return pl.pallas_call(
        _combine_kernel,
        out_shape=jax.ShapeDtypeStruct((t, d), F32),
        grid=(t // tm,),
        in_specs=[pl.BlockSpec((tm * TOP_K,), lambda i: (i,), memory_space=pltpu.SMEM),
                  pl.BlockSpec((tm, TOP_K), lambda i: (i, 0)),
                  pl.BlockSpec((tm, d), lambda i: (i, 0)),
                  pl.BlockSpec((1, d), lambda i: (0, 0)),
                  pl.BlockSpec(memory_space=pl.ANY)],
        out_specs=pl.BlockSpec((tm, d), lambda i: (i, 0)),
        scratch_shapes=[pltpu.VMEM((TOP_K, tm, d), F32),
                        pltpu.SemaphoreType.DMA(())],
        compiler_params=_params("arbitrary"),
    )(pos_flat, gates, h, norm_w.reshape(1, d), ys)


def _tiles(t, s, d, f):
    return dict(
        norm_rows=min(256, t),
        mm_rows=min(1024, t),
        mm_cols=min(1024, d),
        attn_q=min(256, s),
        router_rows=min(256, t),
        rank_rows=min(512, t),
        dispatch_rows=min(256, t),
        expert_rows=min(512, t),
        ff_chunk=min(512, f),
        combine_rows=min(256, t),
    )


def _alibi_slopes(n):
    return jnp.exp2(-ALIBI_MAX_BIAS * jnp.arange(1, n + 1, dtype=F32) / n)


def _layer(h, l, norm1_w, w_in, lam_q1, lam_k1, lam_q2, lam_k2, subln_w, w_out, norm2_w,
           w_router, b_router, w_gate_up, b_gate_up, w_down, b_down, out_norm_w):
    b, s, d = h.shape
    t = b * s
    n_exp = w_router.shape[-1]
    f = w_down.shape[-2]
    n_heads = d // 2 // HEAD_DIM
    width = n_heads * HEAD_DIM
    tl = _tiles(t, s, d, f)
    lambda_init = 0.8 - 0.6 * math.exp(-0.3 * l)
    x2 = h.reshape(t, d)

    n1 = _rmsnorm(x2, norm1_w[l], tl["norm_rows"])
    proj = _matmul(n1, w_in[l].astype(BF16), tl["mm_rows"], tl["mm_cols"], BF16)
    proj3 = proj.reshape(b, s, proj.shape[1])
    slopes = _alibi_slopes(n_heads)
    lam4 = jnp.stack([lam_q1[l], lam_k1[l], lam_q2[l], lam_k2[l]]).astype(F32)
    blocks = width // HEAD_DIM
    y_a = _diff_attention(proj3, slopes, lam4, subln_w[l].reshape(1, HEAD_DIM).astype(F32),
                          n_heads, 0, blocks, 2 * blocks, tl["attn_q"], lambda_init)
    y_b = _moba_attention(proj3, slopes, n_heads, 3 * blocks, 4 * blocks, 5 * blocks)
    h1 = _outproj(y_a.reshape(t, width), y_b.reshape(t, width), w_out[l].astype(BF16), x2,
                  tl["mm_rows"], tl["mm_cols"])

    xp, ids, gates = _router(h1, norm2_w[l], w_router[l], b_router[l], tl["router_rows"])
    rank, counts = _rank(ids, n_exp, tl["rank_rows"])
    tm = tl["expert_rows"]
    counts = counts.reshape(n_exp).astype(jnp.int32)
    padded = ((counts + tm - 1) // tm) * tm
    ends = jnp.cumsum(padded)
    starts = ends - padded
    onehot = ids[..., None] == jnp.arange(n_exp, dtype=jnp.int32)
    pos = rank + jnp.sum(jnp.where(onehot, starts, 0), axis=-1)
    pos_flat = pos.reshape(t * TOP_K)
    n_tiles = (t * TOP_K) // tm + n_exp
    tile_start = jnp.arange(n_tiles, dtype=jnp.int32) * tm
    tile_valid = (tile_start < ends[-1]).astype(jnp.int32)
    tile_expert = jnp.minimum(jnp.searchsorted(ends, tile_start, side="right"),
                              n_exp - 1).astype(jnp.int32)

    buf = jnp.zeros((n_tiles * tm, d // 2), jnp.uint32)
    xs = _dispatch(pos_flat, xp, buf, tl["dispatch_rows"])
    wgu = w_gate_up[l].reshape(n_exp, d, f, 2)
    wg = wgu[..., 0].astype(BF16)
    wu = wgu[..., 1].astype(BF16)
    bgu = b_gate_up[l].reshape(n_exp, 1, f, 2)
    act = _gateup(tile_expert, tile_valid, xs, wg, wu, bgu[..., 0], bgu[..., 1], tm,
                  tl["ff_chunk"])
    ys = _down(tile_expert, tile_valid, act, w_down[l].astype(BF16),
               b_down[l].reshape(n_exp, 1, d), tm)
    return _combine(pos_flat, gates, h1, out_norm_w, ys, tl["combine_rows"]).reshape(b, s, d)


def kernel(x, norm1_w, w_in, lam_q1, lam_k1, lam_q2, lam_k2, subln_w, w_out, norm2_w,
           w_router, b_router, w_gate_up, b_gate_up, w_down, b_down, final_norm_w):
    depth = w_in.shape[0]
    assert depth == 1, "the fused combine + final norm stage assumes a single layer"
    return _layer(x, 0, norm1_w, w_in, lam_q1, lam_k1, lam_q2, lam_k2, subln_w, w_out,
                  norm2_w, w_router, b_router, w_gate_up, b_gate_up, w_down, b_down,
                  final_norm_w)
```

```python
import functools
import math

import jax
import jax.numpy as jnp
from jax import lax
from jax.experimental import pallas as pl
from jax.experimental.pallas import tpu as pltpu

F32 = jnp.float32
BF16 = jnp.bfloat16

HEAD_DIM = 128
DIFF_QK_DIM = HEAD_DIM // 2
MOBA_BLOCK = 256
MOBA_TOPK = 3
TOP_K = 4
SWIGLU_LIMIT = 7.0
SWIGLU_ALPHA = 1.702
NORM_EPS = 1e-5
ALIBI_MAX_BIAS = 8.0

V7X_VMEM_BYTES = 64 * 1024 * 1024
VMEM_LIMIT_BYTES = V7X_VMEM_BYTES - 8 * 1024 * 1024
V7X_MXU_DIM = 256
SUBLANES = 8
DMA_ISSUE_UNROLL = 8


def _params(*semantics):
    return pltpu.CompilerParams(dimension_semantics=semantics,
                                vmem_limit_bytes=VMEM_LIMIT_BYTES)


def _rmsnorm_kernel(x_ref, w_ref, o_ref):
    x = x_ref[...]
    ms = jnp.mean(x * x, axis=-1, keepdims=True)
    o_ref[...] = ((x * lax.rsqrt(ms + NORM_EPS)) * w_ref[...]).astype(o_ref.dtype)


def _rmsnorm(x, w, tm):
    t, d = x.shape
    return pl.pallas_call(
        _rmsnorm_kernel,
        out_shape=jax.ShapeDtypeStruct((t, d), BF16),
        grid=(t // tm,),
        in_specs=[pl.BlockSpec((tm, d), lambda i: (i, 0)),
                  pl.BlockSpec((1, d), lambda i: (0, 0))],
        out_specs=pl.BlockSpec((tm, d), lambda i: (i, 0)),
        compiler_params=_params("parallel"),
        name="rmsnorm1",
    )(x, w.reshape(1, d))


def _matmul_kernel(a_ref, b_ref, o_ref):
    o_ref[...] = jnp.dot(a_ref[...], b_ref[...],
                         preferred_element_type=F32).astype(o_ref.dtype)


def _matmul(a, b, tm, tn, out_dtype):
    m, k = a.shape
    _, n = b.shape
    return pl.pallas_call(
        _matmul_kernel,
        out_shape=jax.ShapeDtypeStruct((m, n), out_dtype),
        grid=(m // tm, n // tn),
        in_specs=[pl.BlockSpec((tm, k), lambda i, j: (i, 0)),
                  pl.BlockSpec((k, tn), lambda i, j: (0, j))],
        out_specs=pl.BlockSpec((tm, tn), lambda i, j: (i, j)),
        compiler_params=_params("parallel", "parallel"),
        name="in_proj",
    )(a, b)


def _outproj_kernel(ya_ref, yb_ref, w_ref, x_ref, o_ref):
    ka = ya_ref.shape[1]
    acc = jnp.dot(ya_ref[...], w_ref[:ka, :], preferred_element_type=F32)
    acc += jnp.dot(yb_ref[...], w_ref[ka:, :], preferred_element_type=F32)
    o_ref[...] = x_ref[...] + acc


def _outproj(ya, yb, w, x, tm, tn):
    m, ka = ya.shape
    kb = yb.shape[1]
    n = w.shape[1]
    return pl.pallas_call(
        _outproj_kernel,
        out_shape=jax.ShapeDtypeStruct((m, n), F32),
        grid=(m // tm, n // tn),
        in_specs=[pl.BlockSpec((tm, ka), lambda i, j: (i, 0)),
                  pl.BlockSpec((tm, kb), lambda i, j: (i, 0)),
                  pl.BlockSpec((ka + kb, tn), lambda i, j: (0, j)),
                  pl.BlockSpec((tm, tn), lambda i, j: (i, j))],
        out_specs=pl.BlockSpec((tm, tn), lambda i, j: (i, j)),
        compiler_params=_params("parallel", "parallel"),
        name="out_proj",
    )(ya, yb, w, x)


RUNNING_MAX_INIT = -1e30


def _qk(q, k):
    return lax.dot_general(q, k, (((1,), (1,)), ((), ())), preferred_element_type=F32)


def _pair_tables(n_tiles):
    qi = [i for i in range(n_tiles) for _ in range(i + 1)]
    kj = [j for i in range(n_tiles) for j in range(i + 1)]
    return jnp.asarray(qi, jnp.int32), jnp.asarray(kj, jnp.int32)


def _fill_vext(vext_ref, v_ref):
    vext_ref[:, :HEAD_DIM] = v_ref[...]
    vext_ref[:, HEAD_DIM:] = jnp.ones((v_ref.shape[0], HEAD_DIM), vext_ref.dtype)


def _fill_tile_bias(bias_ref, col_bias_scale, t):
    row = lax.broadcasted_iota(jnp.int32, (t, t), 0)
    col = lax.broadcasted_iota(jnp.int32, (t, t), 1)
    cb = col_bias_scale * col.astype(F32)
    bias_ref[0] = cb
    bias_ref[1] = jnp.where(col <= row, cb, -jnp.inf)


def _attn_scratch(s_len, t, n_comp):
    return [pltpu.VMEM((s_len, 2 * HEAD_DIM), BF16),
            pltpu.VMEM((2, t, t), F32),
            pltpu.VMEM((n_comp, t, t), F32),
            pltpu.VMEM((n_comp, t, t), F32),
            pltpu.VMEM((n_comp, t, t), BF16),
            pltpu.VMEM((n_comp, t, t), BF16),
            pltpu.VMEM((n_comp, t, 1), F32),
            pltpu.VMEM((n_comp, t, 1), F32),
            pltpu.VMEM((n_comp, t, 1), F32),
            pltpu.VMEM((n_comp, t, 2 * HEAD_DIM), F32)]


def _attn_pipeline(qi_ref, kj_ref, n_pairs, t, n_comp, scores, score_scale, shift_step, exp_fn,
                   rows_can_be_empty, vext_ref, bufs, finalize):
    s_bufs, p_bufs, a_bufs, m_ref, acc_ref = bufs
    for buf in s_bufs + p_bufs:
        buf[...] = jnp.zeros(buf.shape, buf.dtype)
    for buf in a_bufs:
        buf[...] = jnp.ones(buf.shape, buf.dtype)
    m_ref[...] = jnp.full(m_ref.shape, RUNNING_MAX_INIT, F32)
    acc_ref[...] = jnp.zeros(acc_ref.shape, F32)

    def pair(n):
        nc = jnp.clip(n, 0, n_pairs - 1)
        return qi_ref[nc], kj_ref[nc], (n >= 0) & (n < n_pairs)

    def stage_scores(n, s_buf):
        i, j, _ = pair(n)
        for c in range(n_comp):
            s_buf[c] = scores(i, j, c)

    def stage_softmax(n, s_buf, p_buf, a_buf):
        _, j, _ = pair(n)
        first = j == 0
        max_cap = jnp.where(first, RUNNING_MAX_INIT, jnp.inf)
        keep = jnp.where(first, 0.0, 1.0)
        for c in range(n_comp):
            m_prev = jnp.minimum(m_ref[c], max_cap)
            row_max = jnp.max(s_buf[c], axis=-1, keepdims=True)
            if score_scale != 1.0:
                row_max = row_max * score_scale
            m_cur = jnp.maximum(m_prev, row_max)
            m_ref[c] = m_cur - shift_step
            s = s_buf[c] if score_scale == 1.0 else s_buf[c] * score_scale
            p_buf[c] = exp_fn(s - m_cur).astype(p_buf.dtype)
            rescale = exp_fn(m_prev - m_cur)
            a_buf[c] = rescale * keep if rows_can_be_empty else rescale

    def stage_values(n, p_buf, a_buf):
        i, j, active = pair(n)
        vext = vext_ref[pl.ds(pl.multiple_of(j * t, t), t), :]
        for c in range(n_comp):
            acc_ref[c] = a_buf[c] * acc_ref[c] + jnp.dot(p_buf[c], vext,
                                                         preferred_element_type=F32)
        pl.when(active & (i == j))(lambda: finalize(i))

    def body(step, carry):
        for parity in range(2):
            n = 2 * step + parity
            stage_scores(n, s_bufs[parity])
            stage_softmax(n - 1, s_bufs[1 - parity], p_bufs[1 - parity], a_bufs[1 - parity])
            stage_values(n - 2, p_bufs[parity], a_bufs[parity])
        return carry

    lax.fori_loop(0, (n_pairs + 3) // 2, body, 0)


def _split_bufs(scratch):
    s0, s1, p0, p1, a0, a1, m_ref, acc_ref = scratch
    return (s0, s1), (p0, p1), (a0, a1), m_ref, acc_ref


def _diff_attn_kernel(slopes_ref, qi_ref, kj_ref, lam_ref, subw_ref, q_ref, k_ref, v_ref, o_ref,
                      qc_ref, vext_ref, bias_ref, *scratch, t, n_pairs, lambda_init):
    slope = slopes_ref[pl.program_id(1)]
    dq = DIFF_QK_DIM
    _fill_vext(vext_ref, v_ref)
    _fill_tile_bias(bias_ref, slope, t)

    q_all = q_ref[...] * jnp.asarray(dq ** -0.5, q_ref.dtype)
    lane = lax.broadcasted_iota(jnp.int32, q_all.shape, 1)
    qc_ref[0] = jnp.where(lane < dq, q_all, jnp.zeros_like(q_all))
    qc_ref[1] = jnp.where(lane >= dq, q_all, jnp.zeros_like(q_all))

    def scores(i, j, c):
        q = qc_ref[c, pl.ds(pl.multiple_of(i * t, t), t), :]
        k = k_ref[pl.ds(pl.multiple_of(j * t, t), t), :]
        return _qk(q, k) + bias_ref[(i == j).astype(jnp.int32)]

    bufs = _split_bufs(scratch)
    acc_ref = bufs[-1]

    def finalize(i):
        lam_v = lam_ref[...]
        lam = (jnp.exp(jnp.sum(lam_v[0:1, :] * lam_v[1:2, :], axis=-1, keepdims=True))
               - jnp.exp(jnp.sum(lam_v[2:3, :] * lam_v[3:4, :], axis=-1, keepdims=True))
               + lambda_init)
        a1 = acc_ref[0]
        a2 = acc_ref[1]
        o = (a1[:, :HEAD_DIM] / a1[:, HEAD_DIM:HEAD_DIM + 1]
             - lam * (a2[:, :HEAD_DIM] / a2[:, HEAD_DIM:HEAD_DIM + 1]))
        ms = jnp.mean(o * o, axis=-1, keepdims=True)
        o = (o * lax.rsqrt(ms + NORM_EPS)) * subw_ref[...]
        o_ref[pl.ds(pl.multiple_of(i * t, t), t), :] = (o * (1.0 - lambda_init)).astype(o_ref.dtype)

    _attn_pipeline(qi_ref, kj_ref, n_pairs, t, 2, scores, 1.0, slope * t, jnp.exp, False,
                   vext_ref, bufs, finalize)


def _head_spec(s, blk):
    return pl.BlockSpec((None, s, HEAD_DIM), lambda bi, h: (bi, 0, blk + h))


def _diff_attention(proj3, slopes, lam4, subw, n_heads, q_blk, k_blk, v_blk, t, lambda_init):
    b, s, _ = proj3.shape
    qi, kj = _pair_tables(s // t)
    kern = functools.partial(_diff_attn_kernel, t=t, n_pairs=qi.shape[0], lambda_init=lambda_init)
    smem = pl.BlockSpec(memory_space=pltpu.SMEM)
    return pl.pallas_call(
        kern,
        out_shape=jax.ShapeDtypeStruct((b, s, n_heads * HEAD_DIM), BF16),
        grid=(b, n_heads),
        in_specs=[smem, smem, smem,
                  pl.BlockSpec((4, DIFF_QK_DIM), lambda bi, h: (0, 0)),
                  pl.BlockSpec((1, HEAD_DIM), lambda bi, h: (0, 0)),
                  _head_spec(s, q_blk), _head_spec(s, k_blk), _head_spec(s, v_blk)],
        out_specs=_head_spec(s, 0),
        scratch_shapes=[pltpu.VMEM((2, s, HEAD_DIM), BF16)] + _attn_scratch(s, t, 2),
        compiler_params=_params("parallel", "parallel"),
        name="diff_attention",
    )(slopes, qi, kj, lam4, subw, proj3, proj3, proj3)


MASKED_SCORE = -(2.0 ** 127)
LOG2_E = 1.4426950408889634


def _moba_kernel(slopes_ref, qi_ref, kj_ref, q_ref, k_ref, v_ref, o_ref,
                 kmean_ref, qaug_ref, kaug_ref, vext_ref, bias_ref, *scratch,
                 t, n_pairs, n_blocks):
    slope = slopes_ref[pl.program_id(1)]
    blk = MOBA_BLOCK
    blk_shift = blk.bit_length() - 1
    s_len = q_ref.shape[0]
    scale = HEAD_DIM ** -0.5
    to_log2 = scale * LOG2_E

    _fill_vext(vext_ref, v_ref)
    _fill_tile_bias(bias_ref, slope / scale, t)

    kmean_ref[...] = jnp.zeros(kmean_ref.shape, F32)
    for n in range(n_blocks):
        kb = k_ref[n * blk:(n + 1) * blk, :].astype(F32)
        kmean_ref[n:n + 1, :] = jnp.mean(kb, axis=0, keepdims=True)
    km = kmean_ref[...]
    km_hi = km.astype(BF16)
    km_lo = (km - km_hi.astype(F32)).astype(BF16)

    nb_pad = -(-n_blocks // SUBLANES) * SUBLANES
    blk_row = lax.broadcasted_iota(jnp.int32, (nb_pad, t), 0)
    blk_rowf = blk_row.astype(F32)
    q_pos = lax.broadcasted_iota(jnp.int32, (1, t), 1)
    lane = lax.broadcasted_iota(jnp.int32, (t, HEAD_DIM), 1)
    row = lax.broadcasted_iota(jnp.int32, (t, 1), 0)
    for it in range(s_len // t):
        rows = slice(it * t, (it + 1) * t)
        q = q_ref[rows, :]
        k = k_ref[rows, :]
        own = (it * t + q_pos) >> blk_shift
        gate = (_qk(km_hi, q) + _qk(km_lo, q))[:nb_pad, :]
        neg = jnp.full_like(gate, -jnp.inf)
        g = jnp.where(blk_row < own, gate, neg)
        visible = blk_row == own
        for _ in range(min(MOBA_TOPK, n_blocks)):
            mx = jnp.max(g, axis=0, keepdims=True)
            first = jnp.min(jnp.where(g == mx, blk_rowf, float(nb_pad)), axis=0, keepdims=True)
            pick = (blk_rowf == first) & (mx > -jnp.inf)
            visible = visible | pick
            g = jnp.where(pick, neg, g)
        bias_t = jnp.where(visible | (blk_row >= n_blocks), 0.0, MASKED_SCORE)
        bias_t = jnp.concatenate([bias_t, jnp.zeros((HEAD_DIM - nb_pad, t), F32)], axis=0)
        qaug_ref[rows, :] = jnp.concatenate([q, bias_t.T.astype(BF16)], axis=1)
        own_col = (it * t + row) >> blk_shift
        kaug_ref[rows, :] = jnp.concatenate([k, (lane == own_col).astype(BF16)], axis=1)

    def scores(i, j, c):
        q = qaug_ref[pl.ds(pl.multiple_of(i * t, t), t), :]
        k = kaug_ref[pl.ds(pl.multiple_of(j * t, t), t), :]
        return _qk(q, k) + bias_ref[(i == j).astype(jnp.int32)]

    bufs = _split_bufs(scratch)
    acc_ref = bufs[-1]

    def finalize(i):
        acc = acc_ref[0]
        o_ref[pl.ds(pl.multiple_of(i * t, t), t), :] = (
            acc[:, :HEAD_DIM] / acc[:, HEAD_DIM:HEAD_DIM + 1]).astype(o_ref.dtype)

    _attn_pipeline(qi_ref, kj_ref, n_pairs, t, 1, scores, to_log2, LOG2_E * slope * t, jnp.exp2,
                   True, vext_ref, bufs, finalize)


def _moba_attention(proj3, slopes, n_heads, q_blk, k_blk, v_blk, t):
    b, s, _ = proj3.shape
    n_blocks = s // MOBA_BLOCK
    assert n_blocks <= HEAD_DIM, "one lane per key block"
    qi, kj = _pair_tables(s // t)
    kern = functools.partial(_moba_kernel, t=t, n_pairs=qi.shape[0], n_blocks=n_blocks)
    smem = pl.BlockSpec(memory_space=pltpu.SMEM)
    return pl.pallas_call(
        kern,
        out_shape=jax.ShapeDtypeStruct((b, s, n_heads * HEAD_DIM), BF16),
        grid=(b, n_heads),
        in_specs=[smem, smem, smem,
                  _head_spec(s, q_blk), _head_spec(s, k_blk), _head_spec(s, v_blk)],
        out_specs=_head_spec(s, 0),
        scratch_shapes=[pltpu.VMEM((HEAD_DIM, HEAD_DIM), F32),
                        pltpu.VMEM((s, 2 * HEAD_DIM), BF16),
                        pltpu.VMEM((s, 2 * HEAD_DIM), BF16)] + _attn_scratch(s, t, 1),
        compiler_params=_params("parallel", "parallel"),
        name="moba_attention",
    )(slopes, qi, kj, proj3, proj3, proj3)


def _pack_cols(a, b):
    ua = lax.bitcast_convert_type(a, jnp.uint32)
    ub = lax.bitcast_convert_type(b, jnp.uint32)
    return (ua >> 16) | (ub & jnp.uint32(0xFFFF0000))


def _unpack_cols(w):
    lo = lax.bitcast_convert_type(w << 16, F32)
    hi = lax.bitcast_convert_type(w & jnp.uint32(0xFFFF0000), F32)
    return lo.astype(BF16), hi.astype(BF16)


def _lanes_from_columns(cols, dtype):
    tm = cols[0].shape[0]
    lane = lax.broadcasted_iota(jnp.int32, (tm, len(cols)), 1)
    out = jnp.zeros((tm, len(cols)), dtype)
    for r, c in enumerate(cols):
        out = jnp.where(lane == r, c.astype(dtype), out)
    return out


def _router_kernel(h_ref, w_ref, wr_ref, br_ref, xp_ref, ids_ref, gates_ref):
    x = h_ref[...]
    ms = jnp.mean(x * x, axis=-1, keepdims=True)
    n2 = (x * lax.rsqrt(ms + NORM_EPS)) * w_ref[...]
    hi = n2.astype(BF16)
    hi_f = hi.astype(F32)
    half = n2.shape[1] // 2
    xp_ref[...] = _pack_cols(hi_f[:, :half], hi_f[:, half:])

    lo = (n2 - hi_f).astype(BF16)
    wr = wr_ref[...]
    wr_hi = wr.astype(BF16)
    wr_lo = (wr - wr_hi.astype(F32)).astype(BF16)
    logits = (jnp.dot(hi, wr_hi, preferred_element_type=F32)
              + jnp.dot(lo, wr_hi, preferred_element_type=F32)
              + jnp.dot(hi, wr_lo, preferred_element_type=F32)) + br_ref[...]

    n_exp = logits.shape[1]
    eid = lax.broadcasted_iota(jnp.int32, logits.shape, 1).astype(F32)
    neg = jnp.full_like(logits, -jnp.inf)
    g = logits
    vals, idxs = [], []
    for _ in range(TOP_K):
        mx = jnp.max(g, axis=-1, keepdims=True)
        first = jnp.min(jnp.where(g == mx, eid, float(n_exp)), axis=-1, keepdims=True)
        vals.append(mx)
        idxs.append(first)
        g = jnp.where(eid == first, neg, g)
    exps = [jnp.exp(v - vals[0]) for v in vals]
    denom = exps[0] + exps[1] + exps[2] + exps[3]
    ids_ref[...] = _lanes_from_columns(idxs, jnp.int32)
    gates_ref[...] = _lanes_from_columns([e / denom for e in exps], F32)


def _router(h, norm_w, w_router, b_router, tm):
    t, d = h.shape
    n_exp = w_router.shape[1]
    return pl.pallas_call(
        _router_kernel,
        out_shape=(jax.ShapeDtypeStruct((t, d // 2), jnp.uint32),
                   jax.ShapeDtypeStruct((t, TOP_K), jnp.int32),
                   jax.ShapeDtypeStruct((t, TOP_K), F32)),
        grid=(t // tm,),
        in_specs=[pl.BlockSpec((tm, d), lambda i: (i, 0)),
                  pl.BlockSpec((1, d), lambda i: (0, 0)),
                  pl.BlockSpec((d, n_exp), lambda i: (0, 0)),
                  pl.BlockSpec((1, n_exp), lambda i: (0, 0))],
        out_specs=(pl.BlockSpec((tm, d // 2), lambda i: (i, 0)),
                   pl.BlockSpec((tm, TOP_K), lambda i: (i, 0)),
                   pl.BlockSpec((tm, TOP_K), lambda i: (i, 0))),
        compiler_params=_params("parallel"),
        name="rmsnorm2_router",
    )(h, norm_w.reshape(1, d), w_router, b_router.reshape(1, n_exp))


def _rank_kernel(ids_ref, rank_ref, counts_ref, carry_ref, *, n_exp):
    i = pl.program_id(0)

    @pl.when(i == 0)
    def _():
        carry_ref[...] = jnp.zeros_like(carry_ref)

    ids = ids_ref[...]
    tm = ids.shape[0]
    eid = lax.broadcasted_iota(jnp.int32, (tm, n_exp), 1)
    onehots = [(eid == ids[:, r:r + 1]).astype(F32) for r in range(TOP_K)]
    total = onehots[0] + onehots[1] + onehots[2] + onehots[3]
    row = lax.broadcasted_iota(jnp.int32, (tm, tm), 0)
    col = lax.broadcasted_iota(jnp.int32, (tm, tm), 1)
    strict_lower = (col < row).astype(BF16)
    before = jnp.dot(strict_lower, total.astype(BF16), preferred_element_type=F32)
    before = before + carry_ref[...]
    ranks = [jnp.sum(oh * before, axis=-1, keepdims=True) for oh in onehots]
    rank_ref[...] = _lanes_from_columns(ranks, jnp.int32)
    carry_ref[...] += jnp.sum(total, axis=0, keepdims=True)
    counts_ref[...] = carry_ref[...]


def _rank(ids, n_exp, tm):
    t = ids.shape[0]
    return pl.pallas_call(
        functools.partial(_rank_kernel, n_exp=n_exp),
        out_shape=(jax.ShapeDtypeStruct((t, TOP_K), jnp.int32),
                   jax.ShapeDtypeStruct((1, n_exp), F32)),
        grid=(t // tm,),
        in_specs=[pl.BlockSpec((tm, TOP_K), lambda i: (i, 0))],
        out_specs=(pl.BlockSpec((tm, TOP_K), lambda i: (i, 0)),
                   pl.BlockSpec((1, n_exp), lambda i: (0, 0))),
        scratch_shapes=[pltpu.VMEM((1, n_exp), F32)],
        compiler_params=_params("arbitrary"),
        name="expert_rank",
    )(ids)


def _dispatch_kernel(fill_start_ref, fill_n_ref, pos_ref, x_ref, buf_ref,
                     zeros_ref, sem, zsem, *, n_exp, tile_rows):
    i = pl.program_id(0)
    tm = x_ref.shape[0]
    zrows = zeros_ref.shape[0]
    sub = SUBLANES
    n_bits = (zrows // sub - 1).bit_length()

    def head_copy(e, r):
        a = fill_start_ref[e]
        n_head = jnp.minimum((-a) & (sub - 1), fill_n_ref[e])
        return r < n_head, pltpu.make_async_copy(
            zeros_ref.at[pl.ds(0, 1), :], buf_ref.at[pl.ds(a + r, 1), :], zsem)

    def body_copy(e, bit):
        a = fill_start_ref[e]
        n_head = jnp.minimum((-a) & (sub - 1), fill_n_ref[e])
        groups = (fill_n_ref[e] - n_head) // sub
        size = sub << bit
        dst = pl.multiple_of(a + n_head + sub * (groups & ((1 << bit) - 1)), sub)
        return (groups & (1 << bit)) != 0, pltpu.make_async_copy(
            zeros_ref.at[pl.ds(0, size), :], buf_ref.at[pl.ds(dst, size), :], zsem)

    def fill_copies():
        for e in range(n_exp):
            for r in range(sub - 1):
                yield head_copy(e, r)
            for bit in range(n_bits):
                yield body_copy(e, bit)

    def tail_copy(c):
        dst = pl.multiple_of(c * zrows, zrows)
        return pltpu.make_async_copy(zeros_ref, buf_ref.at[pl.ds(dst, zrows), :], zsem)

    @pl.when(i == 0)
    def _():
        zeros_ref[...] = jnp.zeros(zeros_ref.shape, zeros_ref.dtype)
        first_tail = fill_start_ref[n_exp] // zrows
        n_chunks = buf_ref.shape[0] // zrows

        def tail_start(c, carry):
            tail_copy(c).start()
            return carry

        def tail_wait(c, carry):
            tail_copy(c).wait()
            return carry

        for live, cp in fill_copies():
            pl.when(live)(cp.start)
        lax.fori_loop(first_tail, n_chunks, tail_start, 0)
        for live, cp in fill_copies():
            pl.when(live)(cp.wait)
        lax.fori_loop(first_tail, n_chunks, tail_wait, 0)

    def start(r, carry):
        pltpu.make_async_copy(x_ref.at[pl.ds(r // TOP_K, 1), :],
                              buf_ref.at[pl.ds(pos_ref[r], 1), :], sem).start()
        return carry

    lax.fori_loop(0, tm * TOP_K, start, 0, unroll=DMA_ISSUE_UNROLL)
    for _ in range(TOP_K):
        pltpu.make_async_copy(x_ref, buf_ref.at[pl.ds(0, tm), :], sem).wait()


def _dispatch(fill_start, fill_n, pos_flat, xp, n_rows, tm, tile_rows):
    t, w = xp.shape
    n_exp = fill_n.shape[0]
    zrows = tile_rows
    grid_spec = pltpu.PrefetchScalarGridSpec(
        num_scalar_prefetch=2,
        grid=(t // tm,),
        in_specs=[pl.BlockSpec((tm * TOP_K,), lambda i, fs, fn: (i,), memory_space=pltpu.SMEM),
                  pl.BlockSpec((tm, w), lambda i, fs, fn: (i, 0))],
        out_specs=pl.BlockSpec(memory_space=pl.ANY),
        scratch_shapes=[pltpu.VMEM((zrows, w), xp.dtype),
                        pltpu.SemaphoreType.DMA(()),
                        pltpu.SemaphoreType.DMA(())],
    )
    return pl.pallas_call(
        functools.partial(_dispatch_kernel, n_exp=n_exp, tile_rows=tile_rows),
        out_shape=jax.ShapeDtypeStruct((n_rows, w), xp.dtype),
        grid_spec=grid_spec,
        compiler_params=_params("arbitrary"),
        name="dispatch",
    )(fill_start, fill_n, pos_flat, xp)


def _deinterleave_matrix(n):
    r = lax.broadcasted_iota(jnp.int32, (n, n), 0)
    c = lax.broadcasted_iota(jnp.int32, (n, n), 1)
    src = jnp.where(c < n // 2, 2 * c, 2 * (c - n // 2) + 1)
    return (r == src).astype(BF16)


def _gateup_kernel(te_ref, tv_ref, tf_ref, tr_ref, x_ref, w_ref, bg_ref, bu_ref, o_ref,
                   wg_ref, wu_ref):
    del te_ref, tr_ref
    i = pl.program_id(1)
    grp = V7X_MXU_DIM
    half = grp // 2

    @pl.when(tf_ref[i] == 1)
    def _():
        perm = _deinterleave_matrix(grp)
        for g in range(w_ref.shape[1] // grp):
            w = w_ref[:, g * grp:(g + 1) * grp].astype(BF16)
            wp = jnp.dot(w, perm, preferred_element_type=F32).astype(BF16)
            wg_ref[:, g * half:(g + 1) * half] = wp[:, :half]
            wu_ref[:, g * half:(g + 1) * half] = wp[:, half:]

    @pl.when(tv_ref[i] == 1)
    def _():
        xa, xb = _unpack_cols(x_ref[...])
        k_half = xa.shape[1]

        def proj(w_s, b_ref):
            return (jnp.dot(xa, w_s[:k_half, :], preferred_element_type=F32)
                    + jnp.dot(xb, w_s[k_half:, :], preferred_element_type=F32)
                    + b_ref[...])

        g = jnp.minimum(proj(wg_ref, bg_ref), SWIGLU_LIMIT)
        u = jnp.clip(proj(wu_ref, bu_ref), -SWIGLU_LIMIT, SWIGLU_LIMIT)
        sig = 1.0 / (1.0 + jnp.exp(-(g * SWIGLU_ALPHA)))
        o_ref[...] = ((u + 1.0) * (g * sig)).astype(o_ref.dtype)

    @pl.when(tv_ref[i] == 0)
    def _():
        o_ref[...] = jnp.zeros_like(o_ref)


def _gateup(tile_expert, tile_valid, tile_first, tile_row, xs, w_gu, bg, bu, tm, fc):
    rows, k_half = xs.shape
    n_exp, d, f2 = w_gu.shape
    f = f2 // 2
    n_tiles = rows // tm
    n_chunks = f // fc
    last_chunk = n_chunks - 1

    def w_map(c, i, te, tv, tf, tr):
        return (te[i], 0, jnp.where(tv[i] == 1, c, last_chunk))

    grid_spec = pltpu.PrefetchScalarGridSpec(
        num_scalar_prefetch=4,
        grid=(n_chunks, n_tiles),
        in_specs=[pl.BlockSpec((tm, k_half), lambda c, i, te, tv, tf, tr: (tr[i], 0)),
                  pl.BlockSpec((None, d, 2 * fc), w_map),
                  pl.BlockSpec((None, 1, fc), w_map),
                  pl.BlockSpec((None, 1, fc), w_map)],
        out_specs=pl.BlockSpec((tm, fc), lambda c, i, te, tv, tf, tr: (i, c)),
        scratch_shapes=[pltpu.VMEM((d, fc), BF16), pltpu.VMEM((d, fc), BF16)],
    )
    return pl.pallas_call(
        _gateup_kernel,
        out_shape=jax.ShapeDtypeStruct((rows, f), BF16),
        grid_spec=grid_spec,
        compiler_params=_params("arbitrary", "arbitrary"),
        name="expert_gate_up",
    )(tile_expert, tile_valid, tile_first, tile_row, xs, w_gu, bg, bu)


def _down_kernel(te_ref, tv_ref, tf_ref, a_ref, w_ref, b_ref, o_ref, wb_ref):
    i = pl.program_id(1)

    @pl.when(tf_ref[i] == 1)
    def _():
        wb_ref[...] = w_ref[...].astype(BF16)

    @pl.when(tv_ref[i] == 1)
    def _():
        o_ref[...] = jnp.dot(a_ref[...], wb_ref[...], preferred_element_type=F32) + b_ref[...]

    @pl.when(tv_ref[i] == 0)
    def _():
        o_ref[...] = jnp.zeros_like(o_ref)


def _down(tile_expert, tile_valid, tile_first, act, wd, bd, tm, tn):
    rows, f = act.shape
    n_exp, _, d = wd.shape
    n_chunks = d // tn
    last_chunk = n_chunks - 1

    def w_map(c, i, te, tv, tf):
        return (te[i], 0, jnp.where(tv[i] == 1, c, last_chunk))

    grid_spec = pltpu.PrefetchScalarGridSpec(
        num_scalar_prefetch=3,
        grid=(n_chunks, rows // tm),
        in_specs=[pl.BlockSpec((tm, f), lambda c, i, te, tv, tf: (i, 0)),
                  pl.BlockSpec((None, f, tn), w_map),
                  pl.BlockSpec((None, 1, tn), w_map)],
        out_specs=pl.BlockSpec((tm, tn), lambda c, i, te, tv, tf: (i, c)),
        scratch_shapes=[pltpu.VMEM((f, tn), BF16)],
    )
    return pl.pallas_call(
        _down_kernel,
        out_shape=jax.ShapeDtypeStruct((rows, d), F32),
        grid_spec=grid_spec,
        compiler_params=_params("arbitrary", "arbitrary"),
        name="expert_down",
    )(tile_expert, tile_valid, tile_first, act, wd, bd)


def _combine_kernel(pos_ref, gates_ref, h_ref, w_ref, y_ref, o_ref, rows_ref, sem):
    tm = h_ref.shape[0]

    def start(r, carry):
        pltpu.make_async_copy(y_ref.at[pl.ds(pos_ref[r], 1), :],
                              rows_ref.at[r % TOP_K, pl.ds(r // TOP_K, 1), :], sem).start()
        return carry

    lax.fori_loop(0, tm * TOP_K, start, 0, unroll=DMA_ISSUE_UNROLL)
    for r in range(TOP_K):
        pltpu.make_async_copy(y_ref.at[pl.ds(0, tm), :], rows_ref.at[r], sem).wait()

    gates = gates_ref[...]
    moe = gates[:, 0:1] * rows_ref[0]
    for r in range(1, TOP_K):
        moe += gates[:, r:r + 1] * rows_ref[r]
    x = h_ref[...] + moe
    ms = jnp.mean(x * x, axis=-1, keepdims=True)
    o_ref[...] = (x * lax.rsqrt(ms + NORM_EPS)) * w_ref[...]


def _combine(pos_flat, gates, h, norm_w, ys, tm):
    t, d = h.shape
    return pl.pallas_call(
        _combine_kernel,
        out_shape=jax.ShapeDtypeStruct((t, d), F32),
        grid=(t // tm,),
        in_specs=[pl.BlockSpec((tm * TOP_K,), lambda i: (i,), memory_space=pltpu.SMEM),
                  pl.BlockSpec((tm, TOP_K), lambda i: (i, 0)),
                  pl.BlockSpec((tm, d), lambda i: (i, 0)),
                  pl.BlockSpec((1, d), lambda i: (0, 0)),
                  pl.BlockSpec(memory_space=pl.ANY)],
        out_specs=pl.BlockSpec((tm, d), lambda i: (i, 0)),
        scratch_shapes=[pltpu.VMEM((TOP_K, tm, d), F32),
                        pltpu.SemaphoreType.DMA(())],
        compiler_params=_params("arbitrary"),
        name="combine_final_norm",
    )(pos_flat, gates, h, norm_w.reshape(1, d), ys)


def _tiles(t, s, d, f):
    return dict(
        norm_rows=min(256, t),
        mm_rows=min(1024, t),
        mm_cols=min(1024, d),
        attn_tile=min(512, s),
        router_rows=min(256, t),
        rank_rows=min(512, t),
        dispatch_rows=min(256, t),
        expert_rows=min(512, t),
        ff_chunk=min(384, f),
        down_cols=min(1024, d),
        combine_rows=min(256, t),
    )


def _alibi_slopes(n):
    return jnp.exp2(-ALIBI_MAX_BIAS * jnp.arange(1, n + 1, dtype=F32) / n)


def _routing_tables(ids, rank, counts, tm, n_tiles):
    n_exp = counts.shape[0]
    i32 = jnp.int32
    padded = ((counts + tm - 1) // tm) * tm
    ends = jnp.cumsum(padded)
    starts = ends - padded
    onehot = ids[..., None] == jnp.arange(n_exp, dtype=i32)
    pos = rank + jnp.sum(jnp.where(onehot, starts, 0), axis=-1)
    tile_start = jnp.arange(n_tiles, dtype=i32) * tm
    tile_valid = (tile_start < ends[-1]).astype(i32)
    tile_expert = jnp.minimum(jnp.sum((tile_start[:, None] >= ends[None, :]).astype(i32), axis=1),
                              n_exp - 1)
    prev_expert = jnp.concatenate([jnp.full((1,), -1, i32), tile_expert[:-1]])
    tile_first = tile_valid * (tile_expert != prev_expert).astype(i32)
    tile_row = jnp.minimum(jnp.arange(n_tiles, dtype=i32), ends[-1] // tm - 1)
    fill_start = jnp.concatenate([starts + counts, ends[-1:]]).astype(i32)
    fill_n = (padded - counts).astype(i32)
    return pos.reshape(-1), tile_expert, tile_valid, tile_first, tile_row, fill_start, fill_n


def _layer(h, l, norm1_w, w_in, lam_q1, lam_k1, lam_q2, lam_k2, subln_w, w_out, norm2_w,
           w_router, b_router, w_gate_up, b_gate_up, w_down, b_down, out_norm_w):
    b, s, d = h.shape
    t = b * s
    n_exp = w_router.shape[-1]
    f = w_down.shape[-2]
    n_heads = d // 2 // HEAD_DIM
    width = n_heads * HEAD_DIM
    tl = _tiles(t, s, d, f)
    lambda_init = 0.8 - 0.6 * math.exp(-0.3 * l)
    x2 = h.reshape(t, d)

    n1 = _rmsnorm(x2, norm1_w[l], tl["norm_rows"])
    proj = _matmul(n1, w_in[l].astype(BF16), tl["mm_rows"], tl["mm_cols"], BF16)
    proj3 = proj.reshape(b, s, proj.shape[1])
    slopes = _alibi_slopes(n_heads)
    lam4 = jnp.stack([lam_q1[l], lam_k1[l], lam_q2[l], lam_k2[l]]).astype(F32)
    blocks = width // HEAD_DIM
    y_a = _diff_attention(proj3, slopes, lam4, subln_w[l].reshape(1, HEAD_DIM).astype(F32),
                          n_heads, 0, blocks, 2 * blocks, tl["attn_tile"], lambda_init)
    y_b = _moba_attention(proj3, slopes, n_heads, 3 * blocks, 4 * blocks, 5 * blocks,
                          tl["attn_tile"])
    h1 = _outproj(y_a.reshape(t, width), y_b.reshape(t, width), w_out[l].astype(BF16), x2,
                  tl["mm_rows"], tl["mm_cols"])

    xp, ids, gates = _router(h1, norm2_w[l], w_router[l], b_router[l], tl["router_rows"])
    rank, counts = _rank(ids, n_exp, tl["rank_rows"])
    tm = tl["expert_rows"]
    n_tiles = (t * TOP_K) // tm + n_exp
    (pos_flat, tile_expert, tile_valid, tile_first, tile_row, fill_start, fill_n) = \
        _routing_tables(ids, rank, counts.reshape(n_exp).astype(jnp.int32), tm, n_tiles)

    xs = _dispatch(fill_start, fill_n, pos_flat, xp, n_tiles * tm, tl["dispatch_rows"], tm)
    bgu = b_gate_up[l].reshape(n_exp, 1, f, 2)
    act = _gateup(tile_expert, tile_valid, tile_first, tile_row, xs, w_gate_up[l],
                  bgu[..., 0], bgu[..., 1], tm, tl["ff_chunk"])
    ys = _down(tile_expert, tile_valid, tile_first, act, w_down[l],
               b_down[l].reshape(n_exp, 1, d), tm, tl["down_cols"])
    return _combine(pos_flat, gates, h1, out_norm_w, ys, tl["combine_rows"]).reshape(b, s, d)


def kernel(x, norm1_w, w_in, lam_q1, lam_k1, lam_q2, lam_k2, subln_w, w_out, norm2_w,
           w_router, b_router, w_gate_up, b_gate_up, w_down, b_down, final_norm_w):
    depth = w_in.shape[0]
    assert depth == 1, "the fused combine + final norm stage assumes a single layer"
    return _layer(x, 0, norm1_w, w_in, lam_q1, lam_k1, lam_q2, lam_k2, subln_w, w_out,
                  norm2_w, w_router, b_router, w_gate_up, b_gate_up, w_down, b_down,
                  final_norm_w)
```

```python
import functools
import math

import jax
import jax.numpy as jnp
from jax import lax
from jax.experimental import pallas as pl
from jax.experimental.pallas import tpu as pltpu

F32 = jnp.float32
BF16 = jnp.bfloat16

HEAD_DIM = 128
DIFF_QK_DIM = HEAD_DIM // 2
MOBA_BLOCK = 256
MOBA_TOPK = 3
TOP_K = 4
SWIGLU_LIMIT = 7.0
SWIGLU_ALPHA = 1.702
NORM_EPS = 1e-5
ALIBI_MAX_BIAS = 8.0

V7X_VMEM_BYTES = 64 * 1024 * 1024
VMEM_LIMIT_BYTES = V7X_VMEM_BYTES - 8 * 1024 * 1024
V7X_MXU_DIM = 256
SUBLANES = 8
DMA_ISSUE_UNROLL = 8


def _params(*semantics):
    return pltpu.CompilerParams(dimension_semantics=semantics,
                                vmem_limit_bytes=VMEM_LIMIT_BYTES)


def _rmsnorm_kernel(x_ref, w_ref, o_ref):
    x = x_ref[...]
    ms = jnp.mean(x * x, axis=-1, keepdims=True)
    o_ref[...] = ((x * lax.rsqrt(ms + NORM_EPS)) * w_ref[...]).astype(o_ref.dtype)


def _rmsnorm(x, w, tm):
    t, d = x.shape
    return pl.pallas_call(
        _rmsnorm_kernel,
        out_shape=jax.ShapeDtypeStruct((t, d), BF16),
        grid=(t // tm,),
        in_specs=[pl.BlockSpec((tm, d), lambda i: (i, 0)),
                  pl.BlockSpec((1, d), lambda i: (0, 0))],
        out_specs=pl.BlockSpec((tm, d), lambda i: (i, 0)),
        compiler_params=_params("parallel"),
        name="rmsnorm1",
    )(x, w.reshape(1, d))


def _matmul_kernel(a_ref, b_ref, cs_ref, o_ref):
    acc = jnp.dot(a_ref[...], b_ref[...], preferred_element_type=F32)
    o_ref[...] = (acc * cs_ref[...]).astype(o_ref.dtype)


def _matmul(a, b, col_scale, tm, tn, out_dtype):
    m, k = a.shape
    _, n = b.shape
    return pl.pallas_call(
        _matmul_kernel,
        out_shape=jax.ShapeDtypeStruct((m, n), out_dtype),
        grid=(m // tm, n // tn),
        in_specs=[pl.BlockSpec((tm, k), lambda i, j: (i, 0)),
                  pl.BlockSpec((k, tn), lambda i, j: (0, j)),
                  pl.BlockSpec((1, tn), lambda i, j: (0, j))],
        out_specs=pl.BlockSpec((tm, tn), lambda i, j: (i, j)),
        compiler_params=_params("parallel", "parallel"),
        name="in_proj",
    )(a, b, col_scale)


def _outproj_kernel(ya_ref, yb_ref, w_ref, x_ref, o_ref):
    ka = ya_ref.shape[1]
    acc = jnp.dot(ya_ref[...], w_ref[:ka, :], preferred_element_type=F32)
    acc += jnp.dot(yb_ref[...], w_ref[ka:, :], preferred_element_type=F32)
    o_ref[...] = x_ref[...] + acc


def _outproj(ya, yb, w, x, tm, tn):
    m, ka = ya.shape
    kb = yb.shape[1]
    n = w.shape[1]
    return pl.pallas_call(
        _outproj_kernel,
        out_shape=jax.ShapeDtypeStruct((m, n), F32),
        grid=(m // tm, n // tn),
        in_specs=[pl.BlockSpec((tm, ka), lambda i, j: (i, 0)),
                  pl.BlockSpec((tm, kb), lambda i, j: (i, 0)),
                  pl.BlockSpec((ka + kb, tn), lambda i, j: (0, j)),
                  pl.BlockSpec((tm, tn), lambda i, j: (i, j))],
        out_specs=pl.BlockSpec((tm, tn), lambda i, j: (i, j)),
        compiler_params=_params("parallel", "parallel"),
        name="out_proj",
    )(ya, yb, w, x)


RUNNING_MAX_INIT = -1e30
LOG2_E = 1.4426950408889634


def _qk(q, k):
    return lax.dot_general(q, k, (((1,), (1,)), ((), ())), preferred_element_type=F32)


def _pair_tables(n_tiles):
    qi = [i for i in range(n_tiles) for _ in range(i + 1)]
    kj = [j for i in range(n_tiles) for j in range(i + 1)]
    return jnp.asarray(qi, jnp.int32), jnp.asarray(kj, jnp.int32)


def _fill_vext(vext_ref, v_ref):
    vext_ref[:, :HEAD_DIM] = v_ref[...]
    vext_ref[:, HEAD_DIM:] = jnp.ones((v_ref.shape[0], HEAD_DIM), vext_ref.dtype)


def _fill_tile_bias(bias_ref, col_bias_scale, t):
    row = lax.broadcasted_iota(jnp.int32, (t, t), 0)
    col = lax.broadcasted_iota(jnp.int32, (t, t), 1)
    cb = col_bias_scale * col.astype(F32)
    bias_ref[0] = cb
    bias_ref[1] = jnp.where(col <= row, cb, -jnp.inf)


def _attn_scratch(s_len, t, n_comp):
    return [pltpu.VMEM((s_len, 2 * HEAD_DIM), BF16),
            pltpu.VMEM((2, t, t), F32),
            pltpu.VMEM((n_comp, t, t), F32),
            pltpu.VMEM((n_comp, t, t), F32),
            pltpu.VMEM((n_comp, t, t), BF16),
            pltpu.VMEM((n_comp, t, t), BF16),
            pltpu.VMEM((n_comp, t, 1), F32),
            pltpu.VMEM((n_comp, t, 1), F32),
            pltpu.VMEM((n_comp, t, 1), F32),
            pltpu.VMEM((n_comp, t, 2 * HEAD_DIM), F32)]


def _attn_pipeline(qi_ref, kj_ref, n_pairs, t, n_comp, scores, shift_step, rows_can_be_empty,
                   vext_ref, bufs, finalize):
    s_bufs, p_bufs, a_bufs, m_ref, acc_ref = bufs
    for buf in s_bufs + p_bufs:
        buf[...] = jnp.zeros(buf.shape, buf.dtype)
    for buf in a_bufs:
        buf[...] = jnp.ones(buf.shape, buf.dtype)
    m_ref[...] = jnp.full(m_ref.shape, RUNNING_MAX_INIT, F32)
    acc_ref[...] = jnp.zeros(acc_ref.shape, F32)

    def pair(n):
        nc = jnp.clip(n, 0, n_pairs - 1)
        return qi_ref[nc], kj_ref[nc], (n >= 0) & (n < n_pairs)

    def stage_scores(n, s_buf):
        i, j, _ = pair(n)
        for c in range(n_comp):
            s_buf[c] = scores(i, j, c)

    def stage_softmax(n, s_buf, p_buf, a_buf):
        _, j, _ = pair(n)
        first = j == 0
        max_cap = jnp.where(first, RUNNING_MAX_INIT, jnp.inf)
        keep = jnp.where(first, 0.0, 1.0)
        for c in range(n_comp):
            m_prev = jnp.minimum(m_ref[c], max_cap)
            m_cur = jnp.maximum(m_prev, jnp.max(s_buf[c], axis=-1, keepdims=True))
            m_ref[c] = m_cur - shift_step
            p_buf[c] = jnp.exp2(s_buf[c] - m_cur).astype(p_buf.dtype)
            rescale = jnp.exp2(m_prev - m_cur)
            a_buf[c] = rescale * keep if rows_can_be_empty else rescale

    def stage_values(n, p_buf, a_buf):
        i, j, active = pair(n)
        vext = vext_ref[pl.ds(pl.multiple_of(j * t, t), t), :]
        for c in range(n_comp):
            acc_ref[c] = a_buf[c] * acc_ref[c] + jnp.dot(p_buf[c], vext,
                                                         preferred_element_type=F32)
        return i, active & (i == j)

    def body(step, carry):
        for parity in range(2):
            n = 2 * step + parity
            i_done, tile_done = stage_values(n - 2, p_bufs[parity], a_bufs[parity])
            stage_scores(n, s_bufs[parity])
            stage_softmax(n - 1, s_bufs[1 - parity], p_bufs[1 - parity], a_bufs[1 - parity])
            pl.when(tile_done)(functools.partial(finalize, i_done))
        return carry

    lax.fori_loop(0, (n_pairs + 3) // 2, body, 0)


def _split_bufs(scratch):
    s0, s1, p0, p1, a0, a1, m_ref, acc_ref = scratch
    return (s0, s1), (p0, p1), (a0, a1), m_ref, acc_ref


def _diff_attn_kernel(slopes_ref, qi_ref, kj_ref, lam_ref, subw_ref, q_ref, k_ref, v_ref, o_ref,
                      qc_ref, vext_ref, bias_ref, *scratch, t, n_pairs, lambda_init):
    slope = slopes_ref[pl.program_id(1)]
    dq = DIFF_QK_DIM
    _fill_vext(vext_ref, v_ref)
    _fill_tile_bias(bias_ref, LOG2_E * slope, t)

    q_all = q_ref[...]
    lane = lax.broadcasted_iota(jnp.int32, q_all.shape, 1)
    qc_ref[0] = jnp.where(lane < dq, q_all, jnp.zeros_like(q_all))
    qc_ref[1] = jnp.where(lane >= dq, q_all, jnp.zeros_like(q_all))

    def scores(i, j, c):
        q = qc_ref[c, pl.ds(pl.multiple_of(i * t, t), t), :]
        k = k_ref[pl.ds(pl.multiple_of(j * t, t), t), :]
        return _qk(q, k) + bias_ref[(i == j).astype(jnp.int32)]

    bufs = _split_bufs(scratch)
    acc_ref = bufs[-1]

    def finalize(i):
        lam_v = lam_ref[...]
        lam = (jnp.exp(jnp.sum(lam_v[0:1, :] * lam_v[1:2, :], axis=-1, keepdims=True))
               - jnp.exp(jnp.sum(lam_v[2:3, :] * lam_v[3:4, :], axis=-1, keepdims=True))
               + lambda_init)
        a1 = acc_ref[0]
        a2 = acc_ref[1]
        o = (a1[:, :HEAD_DIM] / a1[:, HEAD_DIM:HEAD_DIM + 1]
             - lam * (a2[:, :HEAD_DIM] / a2[:, HEAD_DIM:HEAD_DIM + 1]))
        ms = jnp.mean(o * o, axis=-1, keepdims=True)
        o = (o * lax.rsqrt(ms + NORM_EPS)) * subw_ref[...]
        o_ref[pl.ds(pl.multiple_of(i * t, t), t), :] = (o * (1.0 - lambda_init)).astype(o_ref.dtype)

    _attn_pipeline(qi_ref, kj_ref, n_pairs, t, 2, scores, LOG2_E * slope * t, False,
                   vext_ref, bufs, finalize)


def _head_spec(s, blk):
    return pl.BlockSpec((None, s, HEAD_DIM), lambda bi, h: (bi, 0, blk + h))


def _diff_attention(proj3, slopes, lam4, subw, n_heads, q_blk, k_blk, v_blk, t, lambda_init):
    b, s, _ = proj3.shape
    qi, kj = _pair_tables(s // t)
    kern = functools.partial(_diff_attn_kernel, t=t, n_pairs=qi.shape[0], lambda_init=lambda_init)
    smem = pl.BlockSpec(memory_space=pltpu.SMEM)
    return pl.pallas_call(
        kern,
        out_shape=jax.ShapeDtypeStruct((b, s, n_heads * HEAD_DIM), BF16),
        grid=(b, n_heads),
        in_specs=[smem, smem, smem,
                  pl.BlockSpec((4, DIFF_QK_DIM), lambda bi, h: (0, 0)),
                  pl.BlockSpec((1, HEAD_DIM), lambda bi, h: (0, 0)),
                  _head_spec(s, q_blk), _head_spec(s, k_blk), _head_spec(s, v_blk)],
        out_specs=_head_spec(s, 0),
        scratch_shapes=[pltpu.VMEM((2, s, HEAD_DIM), BF16)] + _attn_scratch(s, t, 2),
        compiler_params=_params("parallel", "parallel"),
        name="diff_attention",
    )(slopes, qi, kj, lam4, subw, proj3, proj3, proj3)


MASKED_SCORE = -(2.0 ** 127)


def _moba_kernel(slopes_ref, qi_ref, kj_ref, q_ref, k_ref, v_ref, o_ref,
                 kmean_ref, qaug_ref, kaug_ref, vext_ref, bias_ref, *scratch,
                 t, n_pairs, n_blocks):
    slope = slopes_ref[pl.program_id(1)]
    blk = MOBA_BLOCK
    blk_shift = blk.bit_length() - 1
    s_len = q_ref.shape[0]

    _fill_vext(vext_ref, v_ref)
    _fill_tile_bias(bias_ref, LOG2_E * slope, t)

    kmean_ref[...] = jnp.zeros(kmean_ref.shape, F32)
    for n in range(n_blocks):
        kb = k_ref[n * blk:(n + 1) * blk, :].astype(F32)
        kmean_ref[n:n + 1, :] = jnp.mean(kb, axis=0, keepdims=True)
    km = kmean_ref[...]
    km_hi = km.astype(BF16)
    km_lo = (km - km_hi.astype(F32)).astype(BF16)

    nb_pad = -(-n_blocks // SUBLANES) * SUBLANES
    blk_row = lax.broadcasted_iota(jnp.int32, (nb_pad, t), 0)
    blk_rowf = blk_row.astype(F32)
    q_pos = lax.broadcasted_iota(jnp.int32, (1, t), 1)
    lane = lax.broadcasted_iota(jnp.int32, (t, HEAD_DIM), 1)
    row = lax.broadcasted_iota(jnp.int32, (t, 1), 0)
    for it in range(s_len // t):
        rows = slice(it * t, (it + 1) * t)
        q = q_ref[rows, :]
        k = k_ref[rows, :]
        own = (it * t + q_pos) >> blk_shift
        gate = (_qk(km_hi, q) + _qk(km_lo, q))[:nb_pad, :]
        neg = jnp.full_like(gate, -jnp.inf)
        g = jnp.where(blk_row < own, gate, neg)
        visible = blk_row == own
        for _ in range(min(MOBA_TOPK, n_blocks)):
            mx = jnp.max(g, axis=0, keepdims=True)
            first = jnp.min(jnp.where(g == mx, blk_rowf, float(nb_pad)), axis=0, keepdims=True)
            pick = (blk_rowf == first) & (mx > -jnp.inf)
            visible = visible | pick
            g = jnp.where(pick, neg, g)
        bias_t = jnp.where(visible | (blk_row >= n_blocks), 0.0, MASKED_SCORE)
        bias_t = jnp.concatenate([bias_t, jnp.zeros((HEAD_DIM - nb_pad, t), F32)], axis=0)
        qaug_ref[rows, :] = jnp.concatenate([q, bias_t.T.astype(BF16)], axis=1)
        own_col = (it * t + row) >> blk_shift
        kaug_ref[rows, :] = jnp.concatenate([k, (lane == own_col).astype(BF16)], axis=1)

    def scores(i, j, c):
        q = qaug_ref[pl.ds(pl.multiple_of(i * t, t), t), :]
        k = kaug_ref[pl.ds(pl.multiple_of(j * t, t), t), :]
        return _qk(q, k) + bias_ref[(i == j).astype(jnp.int32)]

    bufs = _split_bufs(scratch)
    acc_ref = bufs[-1]

    def finalize(i):
        acc = acc_ref[0]
        o_ref[pl.ds(pl.multiple_of(i * t, t), t), :] = (
            acc[:, :HEAD_DIM] / acc[:, HEAD_DIM:HEAD_DIM + 1]).astype(o_ref.dtype)

    _attn_pipeline(qi_ref, kj_ref, n_pairs, t, 1, scores, LOG2_E * slope * t, True,
                   vext_ref, bufs, finalize)


def _moba_attention(proj3, slopes, n_heads, q_blk, k_blk, v_blk, t):
    b, s, _ = proj3.shape
    n_blocks = s // MOBA_BLOCK
    assert n_blocks <= HEAD_DIM, "one lane per key block"
    qi, kj = _pair_tables(s // t)
    kern = functools.partial(_moba_kernel, t=t, n_pairs=qi.shape[0], n_blocks=n_blocks)
    smem = pl.BlockSpec(memory_space=pltpu.SMEM)
    return pl.pallas_call(
        kern,
        out_shape=jax.ShapeDtypeStruct((b, s, n_heads * HEAD_DIM), BF16),
        grid=(b, n_heads),
        in_specs=[smem, smem, smem,
                  _head_spec(s, q_blk), _head_spec(s, k_blk), _head_spec(s, v_blk)],
        out_specs=_head_spec(s, 0),
        scratch_shapes=[pltpu.VMEM((HEAD_DIM, HEAD_DIM), F32),
                        pltpu.VMEM((s, 2 * HEAD_DIM), BF16),
                        pltpu.VMEM((s, 2 * HEAD_DIM), BF16)] + _attn_scratch(s, t, 1),
        compiler_params=_params("parallel", "parallel"),
        name="moba_attention",
    )(slopes, qi, kj, proj3, proj3, proj3)


def _pack_cols(a, b):
    ua = lax.bitcast_convert_type(a, jnp.uint32)
    ub = lax.bitcast_convert_type(b, jnp.uint32)
    return (ua >> 16) | (ub & jnp.uint32(0xFFFF0000))


def _unpack_cols(w):
    lo = lax.bitcast_convert_type(w << 16, F32)
    hi = lax.bitcast_convert_type(w & jnp.uint32(0xFFFF0000), F32)
    return lo.astype(BF16), hi.astype(BF16)


def _lanes_from_columns(cols, dtype):
    tm = cols[0].shape[0]
    lane = lax.broadcasted_iota(jnp.int32, (tm, len(cols)), 1)
    out = jnp.zeros((tm, len(cols)), dtype)
    for r, c in enumerate(cols):
        out = jnp.where(lane == r, c.astype(dtype), out)
    return out


def _router_kernel(h_ref, w_ref, wr_ref, br_ref, xp_ref, ids_ref, gates_ref):
    x = h_ref[...]
    ms = jnp.mean(x * x, axis=-1, keepdims=True)
    n2 = (x * lax.rsqrt(ms + NORM_EPS)) * w_ref[...]
    hi = n2.astype(BF16)
    hi_f = hi.astype(F32)
    half = n2.shape[1] // 2
    xp_ref[...] = _pack_cols(hi_f[:, :half], hi_f[:, half:])

    lo = (n2 - hi_f).astype(BF16)
    wr = wr_ref[...]
    wr_hi = wr.astype(BF16)
    wr_lo = (wr - wr_hi.astype(F32)).astype(BF16)
    logits = (jnp.dot(hi, wr_hi, preferred_element_type=F32)
              + jnp.dot(lo, wr_hi, preferred_element_type=F32)
              + jnp.dot(hi, wr_lo, preferred_element_type=F32)) + br_ref[...]

    n_exp = logits.shape[1]
    eid = lax.broadcasted_iota(jnp.int32, logits.shape, 1).astype(F32)
    neg = jnp.full_like(logits, -jnp.inf)
    g = logits
    vals, idxs = [], []
    for _ in range(TOP_K):
        mx = jnp.max(g, axis=-1, keepdims=True)
        first = jnp.min(jnp.where(g == mx, eid, float(n_exp)), axis=-1, keepdims=True)
        vals.append(mx)
        idxs.append(first)
        g = jnp.where(eid == first, neg, g)
    exps = [jnp.exp(v - vals[0]) for v in vals]
    denom = exps[0] + exps[1] + exps[2] + exps[3]
    ids_ref[...] = _lanes_from_columns(idxs, jnp.int32)
    gates_ref[...] = _lanes_from_columns([e / denom for e in exps], F32)


def _router(h, norm_w, w_router, b_router, tm):
    t, d = h.shape
    n_exp = w_router.shape[1]
    return pl.pallas_call(
        _router_kernel,
        out_shape=(jax.ShapeDtypeStruct((t, d // 2), jnp.uint32),
                   jax.ShapeDtypeStruct((t, TOP_K), jnp.int32),
                   jax.ShapeDtypeStruct((t, TOP_K), F32)),
        grid=(t // tm,),
        in_specs=[pl.BlockSpec((tm, d), lambda i: (i, 0)),
                  pl.BlockSpec((1, d), lambda i: (0, 0)),
                  pl.BlockSpec((d, n_exp), lambda i: (0, 0)),
                  pl.BlockSpec((1, n_exp), lambda i: (0, 0))],
        out_specs=(pl.BlockSpec((tm, d // 2), lambda i: (i, 0)),
                   pl.BlockSpec((tm, TOP_K), lambda i: (i, 0)),
                   pl.BlockSpec((tm, TOP_K), lambda i: (i, 0))),
        compiler_params=_params("parallel"),
        name="rmsnorm2_router",
    )(h, norm_w.reshape(1, d), w_router, b_router.reshape(1, n_exp))


def _rank_kernel(ids_ref, rank_ref, counts_ref, carry_ref, *, n_exp):
    i = pl.program_id(0)

    @pl.when(i == 0)
    def _():
        carry_ref[...] = jnp.zeros_like(carry_ref)

    ids = ids_ref[...]
    tm = ids.shape[0]
    eid = lax.broadcasted_iota(jnp.int32, (tm, n_exp), 1)
    onehots = [(eid == ids[:, r:r + 1]).astype(F32) for r in range(TOP_K)]
    total = onehots[0] + onehots[1] + onehots[2] + onehots[3]
    row = lax.broadcasted_iota(jnp.int32, (tm, tm), 0)
    col = lax.broadcasted_iota(jnp.int32, (tm, tm), 1)
    strict_lower = (col < row).astype(BF16)
    before = jnp.dot(strict_lower, total.astype(BF16), preferred_element_type=F32)
    before = before + carry_ref[...]
    ranks = [jnp.sum(oh * before, axis=-1, keepdims=True) for oh in onehots]
    rank_ref[...] = _lanes_from_columns(ranks, jnp.int32)
    carry_ref[...] += jnp.sum(total, axis=0, keepdims=True)
    counts_ref[...] = carry_ref[...]


def _rank(ids, n_exp, tm):
    t = ids.shape[0]
    return pl.pallas_call(
        functools.partial(_rank_kernel, n_exp=n_exp),
        out_shape=(jax.ShapeDtypeStruct((t, TOP_K), jnp.int32),
                   jax.ShapeDtypeStruct((1, n_exp), F32)),
        grid=(t // tm,),
        in_specs=[pl.BlockSpec((tm, TOP_K), lambda i: (i, 0))],
        out_specs=(pl.BlockSpec((tm, TOP_K), lambda i: (i, 0)),
                   pl.BlockSpec((1, n_exp), lambda i: (0, 0))),
        scratch_shapes=[pltpu.VMEM((1, n_exp), F32)],
        compiler_params=_params("arbitrary"),
        name="expert_rank",
    )(ids)


def _dispatch_kernel(fill_start_ref, fill_n_ref, pos_ref, x_ref, buf_ref,
                     zeros_ref, sem, zsem, *, n_exp, tile_rows):
    i = pl.program_id(0)
    tm = x_ref.shape[0]
    zrows = zeros_ref.shape[0]
    sub = SUBLANES
    n_bits = (zrows // sub - 1).bit_length()

    def head_copy(e, r):
        a = fill_start_ref[e]
        n_head = jnp.minimum((-a) & (sub - 1), fill_n_ref[e])
        return r < n_head, pltpu.make_async_copy(
            zeros_ref.at[pl.ds(0, 1), :], buf_ref.at[pl.ds(a + r, 1), :], zsem)

    def body_copy(e, bit):
        a = fill_start_ref[e]
        n_head = jnp.minimum((-a) & (sub - 1), fill_n_ref[e])
        groups = (fill_n_ref[e] - n_head) // sub
        size = sub << bit
        dst = pl.multiple_of(a + n_head + sub * (groups & ((1 << bit) - 1)), sub)
        return (groups & (1 << bit)) != 0, pltpu.make_async_copy(
            zeros_ref.at[pl.ds(0, size), :], buf_ref.at[pl.ds(dst, size), :], zsem)

    def fill_copies():
        for e in range(n_exp):
            for r in range(sub - 1):
                yield head_copy(e, r)
            for bit in range(n_bits):
                yield body_copy(e, bit)

    def tail_copy(c):
        dst = pl.multiple_of(c * zrows, zrows)
        return pltpu.make_async_copy(zeros_ref, buf_ref.at[pl.ds(dst, zrows), :], zsem)

    @pl.when(i == 0)
    def _():
        zeros_ref[...] = jnp.zeros(zeros_ref.shape, zeros_ref.dtype)
        first_tail = fill_start_ref[n_exp] // zrows
        n_chunks = buf_ref.shape[0] // zrows

        def tail_start(c, carry):
            tail_copy(c).start()
            return carry

        def tail_wait(c, carry):
            tail_copy(c).wait()
            return carry

        for live, cp in fill_copies():
            pl.when(live)(cp.start)
        lax.fori_loop(first_tail, n_chunks, tail_start, 0)
        for live, cp in fill_copies():
            pl.when(live)(cp.wait)
        lax.fori_loop(first_tail, n_chunks, tail_wait, 0)

    def start(r, carry):
        pltpu.make_async_copy(x_ref.at[pl.ds(r // TOP_K, 1), :],
                              buf_ref.at[pl.ds(pos_ref[r], 1), :], sem).start()
        return carry

    lax.fori_loop(0, tm * TOP_K, start, 0, unroll=DMA_ISSUE_UNROLL)
    for _ in range(TOP_K):
        pltpu.make_async_copy(x_ref, buf_ref.at[pl.ds(0, tm), :], sem).wait()


def _dispatch(fill_start, fill_n, pos_flat, xp, n_rows, tm, tile_rows):
    t, w = xp.shape
    n_exp = fill_n.shape[0]
    zrows = tile_rows
    grid_spec = pltpu.PrefetchScalarGridSpec(
        num_scalar_prefetch=2,
        grid=(t // tm,),
        in_specs=[pl.BlockSpec((tm * TOP_K,), lambda i, fs, fn: (i,), memory_space=pltpu.SMEM),
                  pl.BlockSpec((tm, w), lambda i, fs, fn: (i, 0))],
        out_specs=pl.BlockSpec(memory_space=pl.ANY),
        scratch_shapes=[pltpu.VMEM((zrows, w), xp.dtype),
                        pltpu.SemaphoreType.DMA(()),
                        pltpu.SemaphoreType.DMA(())],
    )
    return pl.pallas_call(
        functools.partial(_dispatch_kernel, n_exp=n_exp, tile_rows=tile_rows),
        out_shape=jax.ShapeDtypeStruct((n_rows, w), xp.dtype),
        grid_spec=grid_spec,
        compiler_params=_params("arbitrary"),
        name="dispatch",
    )(fill_start, fill_n, pos_flat, xp)


def _deinterleave_matrix(n):
    r = lax.broadcasted_iota(jnp.int32, (n, n), 0)
    c = lax.broadcasted_iota(jnp.int32, (n, n), 1)
    src = jnp.where(c < n // 2, 2 * c, 2 * (c - n // 2) + 1)
    return (r == src).astype(BF16)


def _gateup_kernel(te_ref, tv_ref, tf_ref, tr_ref, x_ref, w_ref, bg_ref, bu_ref, o_ref,
                   wg_ref, wu_ref):
    del te_ref, tr_ref
    i = pl.program_id(1)
    grp = V7X_MXU_DIM
    half = grp // 2

    @pl.when(tf_ref[i] == 1)
    def _():
        perm = _deinterleave_matrix(grp)
        for g in range(w_ref.shape[1] // grp):
            w = w_ref[:, g * grp:(g + 1) * grp].astype(BF16)
            wp = jnp.dot(w, perm, preferred_element_type=F32).astype(BF16)
            wg_ref[:, g * half:(g + 1) * half] = wp[:, :half]
            wu_ref[:, g * half:(g + 1) * half] = wp[:, half:]

    @pl.when(tv_ref[i] == 1)
    def _():
        xa, xb = _unpack_cols(x_ref[...])
        k_half = xa.shape[1]

        def proj(w_s, b_ref):
            return (jnp.dot(xa, w_s[:k_half, :], preferred_element_type=F32)
                    + jnp.dot(xb, w_s[k_half:, :], preferred_element_type=F32)
                    + b_ref[...])

        g = jnp.minimum(proj(wg_ref, bg_ref), SWIGLU_LIMIT)
        u = jnp.clip(proj(wu_ref, bu_ref), -SWIGLU_LIMIT, SWIGLU_LIMIT)
        sig = 1.0 / (1.0 + jnp.exp(-(g * SWIGLU_ALPHA)))
        o_ref[...] = ((u + 1.0) * (g * sig)).astype(o_ref.dtype)

    @pl.when(tv_ref[i] == 0)
    def _():
        o_ref[...] = jnp.zeros_like(o_ref)


def _gateup(tile_expert, tile_valid, tile_first, tile_row, xs, w_gu, bg, bu, tm, fc):
    rows, k_half = xs.shape
    n_exp, d, f2 = w_gu.shape
    f = f2 // 2
    n_tiles = rows // tm
    n_chunks = f // fc
    last_chunk = n_chunks - 1

    def w_map(c, i, te, tv, tf, tr):
        return (te[i], 0, jnp.where(tv[i] == 1, c, last_chunk))

    grid_spec = pltpu.PrefetchScalarGridSpec(
        num_scalar_prefetch=4,
        grid=(n_chunks, n_tiles),
        in_specs=[pl.BlockSpec((tm, k_half), lambda c, i, te, tv, tf, tr: (tr[i], 0)),
                  pl.BlockSpec((None, d, 2 * fc), w_map),
                  pl.BlockSpec((None, 1, fc), w_map),
                  pl.BlockSpec((None, 1, fc), w_map)],
        out_specs=pl.BlockSpec((tm, fc), lambda c, i, te, tv, tf, tr: (i, c)),
        scratch_shapes=[pltpu.VMEM((d, fc), BF16), pltpu.VMEM((d, fc), BF16)],
    )
    return pl.pallas_call(
        _gateup_kernel,
        out_shape=jax.ShapeDtypeStruct((rows, f), BF16),
        grid_spec=grid_spec,
        compiler_params=_params("arbitrary", "arbitrary"),
        name="expert_gate_up",
    )(tile_expert, tile_valid, tile_first, tile_row, xs, w_gu, bg, bu)


def _down_kernel(te_ref, tv_ref, tf_ref, a_ref, w_ref, b_ref, o_ref, wb_ref):
    i = pl.program_id(1)

    @pl.when(tf_ref[i] == 1)
    def _():
        wb_ref[...] = w_ref[...].astype(BF16)

    @pl.when(tv_ref[i] == 1)
    def _():
        o_ref[...] = jnp.dot(a_ref[...], wb_ref[...], preferred_element_type=F32) + b_ref[...]

    @pl.when(tv_ref[i] == 0)
    def _():
        o_ref[...] = jnp.zeros_like(o_ref)


def _down(tile_expert, tile_valid, tile_first, act, wd, bd, tm, tn):
    rows, f = act.shape
    n_exp, _, d = wd.shape
    n_chunks = d // tn
    last_chunk = n_chunks - 1

    def w_map(c, i, te, tv, tf):
        return (te[i], 0, jnp.where(tv[i] == 1, c, last_chunk))

    grid_spec = pltpu.PrefetchScalarGridSpec(
        num_scalar_prefetch=3,
        grid=(n_chunks, rows // tm),
        in_specs=[pl.BlockSpec((tm, f), lambda c, i, te, tv, tf: (i, 0)),
                  pl.BlockSpec((None, f, tn), w_map),
                  pl.BlockSpec((None, 1, tn), w_map)],
        out_specs=pl.BlockSpec((tm, tn), lambda c, i, te, tv, tf: (i, c)),
        scratch_shapes=[pltpu.VMEM((f, tn), BF16)],
    )
    return pl.pallas_call(
        _down_kernel,
        out_shape=jax.ShapeDtypeStruct((rows, d), F32),
        grid_spec=grid_spec,
        compiler_params=_params("arbitrary", "arbitrary"),
        name="expert_down",
    )(tile_expert, tile_valid, tile_first, act, wd, bd)


def _combine_kernel(pos_ref, gates_ref, h_ref, w_ref, y_ref, o_ref, rows_ref, sem):
    tm = h_ref.shape[0]

    def start(r, carry):
        pltpu.make_async_copy(y_ref.at[pl.ds(pos_ref[r], 1), :],
                              rows_ref.at[r % TOP_K, pl.ds(r // TOP_K, 1), :], sem).start()
        return carry

    lax.fori_loop(0, tm * TOP_K, start, 0, unroll=DMA_ISSUE_UNROLL)
    for r in range(TOP_K):
        pltpu.make_async_copy(y_ref.at[pl.ds(0, tm), :], rows_ref.at[r], sem).wait()

    gates = gates_ref[...]
    moe = gates[:, 0:1] * rows_ref[0]
    for r in range(1, TOP_K):
        moe += gates[:, r:r + 1] * rows_ref[r]
    x = h_ref[...] + moe
    ms = jnp.mean(x * x, axis=-1, keepdims=True)
    o_ref[...] = (x * lax.rsqrt(ms + NORM_EPS)) * w_ref[...]


def _combine(pos_flat, gates, h, norm_w, ys, tm):
    t, d = h.shape
    return pl.pallas_call(
        _combine_kernel,
        out_shape=jax.ShapeDtypeStruct((t, d), F32),
        grid=(t // tm,),
        in_specs=[pl.BlockSpec((tm * TOP_K,), lambda i: (i,), memory_space=pltpu.SMEM),
                  pl.BlockSpec((tm, TOP_K), lambda i: (i, 0)),
                  pl.BlockSpec((tm, d), lambda i: (i, 0)),
                  pl.BlockSpec((1, d), lambda i: (0, 0)),
                  pl.BlockSpec(memory_space=pl.ANY)],
        out_specs=pl.BlockSpec((tm, d), lambda i: (i, 0)),
        scratch_shapes=[pltpu.VMEM((TOP_K, tm, d), F32),
                        pltpu.SemaphoreType.DMA(())],
        compiler_params=_params("arbitrary"),
        name="combine_final_norm",
    )(pos_flat, gates, h, norm_w.reshape(1, d), ys)


def _tiles(t, s, d, f):
    return dict(
        norm_rows=min(256, t),
        mm_rows=min(1024, t),
        mm_cols=min(1024, d),
        attn_tile=min(512, s),
        router_rows=min(256, t),
        rank_rows=min(512, t),
        dispatch_rows=min(256, t),
        expert_rows=min(512, t),
        ff_chunk=min(256, f),
        down_cols=min(2048, d),
        combine_rows=min(256, t),
    )


def _alibi_slopes(n):
    return jnp.exp2(-ALIBI_MAX_BIAS * jnp.arange(1, n + 1, dtype=F32) / n)


def _routing_tables(ids, rank, counts, tm, n_tiles):
    n_exp = counts.shape[0]
    i32 = jnp.int32
    padded = ((counts + tm - 1) // tm) * tm
    ends = jnp.cumsum(padded)
    starts = ends - padded
    onehot = ids[..., None] == jnp.arange(n_exp, dtype=i32)
    pos = rank + jnp.sum(jnp.where(onehot, starts, 0), axis=-1)
    tile_start = jnp.arange(n_tiles, dtype=i32) * tm
    tile_valid = (tile_start < ends[-1]).astype(i32)
    tile_expert = jnp.minimum(jnp.sum((tile_start[:, None] >= ends[None, :]).astype(i32), axis=1),
                              n_exp - 1)
    prev_expert = jnp.concatenate([jnp.full((1,), -1, i32), tile_expert[:-1]])
    tile_first = tile_valid * (tile_expert != prev_expert).astype(i32)
    tile_row = jnp.minimum(jnp.arange(n_tiles, dtype=i32), ends[-1] // tm - 1)
    fill_start = jnp.concatenate([starts + counts, ends[-1:]]).astype(i32)
    fill_n = (padded - counts).astype(i32)
    return pos.reshape(-1), tile_expert, tile_valid, tile_first, tile_row, fill_start, fill_n


def _layer(h, l, norm1_w, w_in, lam_q1, lam_k1, lam_q2, lam_k2, subln_w, w_out, norm2_w,
           w_router, b_router, w_gate_up, b_gate_up, w_down, b_down, out_norm_w):
    b, s, d = h.shape
    t = b * s
    n_exp = w_router.shape[-1]
    f = w_down.shape[-2]
    n_heads = d // 2 // HEAD_DIM
    width = n_heads * HEAD_DIM
    tl = _tiles(t, s, d, f)
    lambda_init = 0.8 - 0.6 * math.exp(-0.3 * l)
    x2 = h.reshape(t, d)

    n1 = _rmsnorm(x2, norm1_w[l], tl["norm_rows"])
    n_cols = w_in.shape[-1]
    col_scale = jnp.ones((n_cols,), F32)
    col_scale = col_scale.at[:width].set(LOG2_E * DIFF_QK_DIM ** -0.5)
    col_scale = col_scale.at[3 * width:4 * width].set(LOG2_E * HEAD_DIM ** -0.5)
    proj = _matmul(n1, w_in[l].astype(BF16), col_scale.reshape(1, n_cols), tl["mm_rows"],
                   tl["mm_cols"], BF16)
    proj3 = proj.reshape(b, s, proj.shape[1])
    slopes = _alibi_slopes(n_heads)
    lam4 = jnp.stack([lam_q1[l], lam_k1[l], lam_q2[l], lam_k2[l]]).astype(F32)
    blocks = width // HEAD_DIM
    y_a = _diff_attention(proj3, slopes, lam4, subln_w[l].reshape(1, HEAD_DIM).astype(F32),
                          n_heads, 0, blocks, 2 * blocks, tl["attn_tile"], lambda_init)
    y_b = _moba_attention(proj3, slopes, n_heads, 3 * blocks, 4 * blocks, 5 * blocks,
                          tl["attn_tile"])
    h1 = _outproj(y_a.reshape(t, width), y_b.reshape(t, width), w_out[l].astype(BF16), x2,
                  tl["mm_rows"], tl["mm_cols"])

    xp, ids, gates = _router(h1, norm2_w[l], w_router[l], b_router[l], tl["router_rows"])
    rank, counts = _rank(ids, n_exp, tl["rank_rows"])
    tm = tl["expert_rows"]
    n_tiles = (t * TOP_K) // tm + n_exp
    (pos_flat, tile_expert, tile_valid, tile_first, tile_row, fill_start, fill_n) = \
        _routing_tables(ids, rank, counts.reshape(n_exp).astype(jnp.int32), tm, n_tiles)

    xs = _dispatch(fill_start, fill_n, pos_flat, xp, n_tiles * tm, tl["dispatch_rows"], tm)
    bgu = b_gate_up[l].reshape(n_exp, 1, f, 2)
    act = _gateup(tile_expert, tile_valid, tile_first, tile_row, xs, w_gate_up[l],
                  bgu[..., 0], bgu[..., 1], tm, tl["ff_chunk"])
    ys = _down(tile_expert, tile_valid, tile_first, act, w_down[l],
               b_down[l].reshape(n_exp, 1, d), tm, tl["down_cols"])
    return _combine(pos_flat, gates, h1, out_norm_w, ys, tl["combine_rows"]).reshape(b, s, d)


def kernel(x, norm1_w, w_in, lam_q1, lam_k1, lam_q2, lam_k2, subln_w, w_out, norm2_w,
           w_router, b_router, w_gate_up, b_gate_up, w_down, b_down, final_norm_w):
    depth = w_in.shape[0]
    assert depth == 1, "the fused combine + final norm stage assumes a single layer"
    return _layer(x, 0, norm1_w, w_in, lam_q1, lam_k1, lam_q2, lam_k2, subln_w, w_out,
                  norm2_w, w_router, b_router, w_gate_up, b_gate_up, w_down, b_down,
                  final_norm_w)
```

```python
import functools
import math

import jax
import jax.numpy as jnp
from jax import lax
from jax.experimental import pallas as pl
from jax.experimental.pallas import tpu as pltpu

F32 = jnp.float32
BF16 = jnp.bfloat16

HEAD_DIM = 128
DIFF_QK_DIM = HEAD_DIM // 2
MOBA_BLOCK = 256
MOBA_TOPK = 3
TOP_K = 4
SWIGLU_LIMIT = 7.0
SWIGLU_ALPHA = 1.702
NORM_EPS = 1e-5
ALIBI_MAX_BIAS = 8.0

V7X_VMEM_BYTES = 64 * 1024 * 1024
VMEM_LIMIT_BYTES = V7X_VMEM_BYTES - 8 * 1024 * 1024
V7X_MXU_DIM = 256
SUBLANES = 8
DMA_ISSUE_UNROLL = 8


def _params(*semantics):
    return pltpu.CompilerParams(dimension_semantics=semantics,
                                vmem_limit_bytes=VMEM_LIMIT_BYTES)


def _rmsnorm_kernel(x_ref, w_ref, o_ref):
    x = x_ref[...]
    ms = jnp.mean(x * x, axis=-1, keepdims=True)
    o_ref[...] = ((x * lax.rsqrt(ms + NORM_EPS)) * w_ref[...]).astype(o_ref.dtype)


def _rmsnorm(x, w, tm):
    t, d = x.shape
    return pl.pallas_call(
        _rmsnorm_kernel,
        out_shape=jax.ShapeDtypeStruct((t, d), BF16),
        grid=(t // tm,),
        in_specs=[pl.BlockSpec((tm, d), lambda i: (i, 0)),
                  pl.BlockSpec((1, d), lambda i: (0, 0))],
        out_specs=pl.BlockSpec((tm, d), lambda i: (i, 0)),
        compiler_params=_params("parallel"),
        name="rmsnorm1",
    )(x, w.reshape(1, d))


def _matmul_kernel(a_ref, b_ref, cs_ref, o_ref):
    acc = jnp.dot(a_ref[...], b_ref[...], preferred_element_type=F32)
    o_ref[...] = (acc * cs_ref[...]).astype(o_ref.dtype)


def _matmul(a, b, col_scale, tm, tn, out_dtype):
    m, k = a.shape
    _, n = b.shape
    return pl.pallas_call(
        _matmul_kernel,
        out_shape=jax.ShapeDtypeStruct((m, n), out_dtype),
        grid=(m // tm, n // tn),
        in_specs=[pl.BlockSpec((tm, k), lambda i, j: (i, 0)),
                  pl.BlockSpec((k, tn), lambda i, j: (0, j)),
                  pl.BlockSpec((1, tn), lambda i, j: (0, j))],
        out_specs=pl.BlockSpec((tm, tn), lambda i, j: (i, j)),
        compiler_params=_params("parallel", "parallel"),
        name="in_proj",
    )(a, b, col_scale)


def _outproj_kernel(ya_ref, yb_ref, w_ref, x_ref, o_ref):
    ka = ya_ref.shape[1]
    acc = jnp.dot(ya_ref[...], w_ref[:ka, :], preferred_element_type=F32)
    acc += jnp.dot(yb_ref[...], w_ref[ka:, :], preferred_element_type=F32)
    o_ref[...] = x_ref[...] + acc


def _outproj(ya, yb, w, x, tm, tn):
    m, ka = ya.shape
    kb = yb.shape[1]
    n = w.shape[1]
    return pl.pallas_call(
        _outproj_kernel,
        out_shape=jax.ShapeDtypeStruct((m, n), F32),
        grid=(m // tm, n // tn),
        in_specs=[pl.BlockSpec((tm, ka), lambda i, j: (i, 0)),
                  pl.BlockSpec((tm, kb), lambda i, j: (i, 0)),
                  pl.BlockSpec((ka + kb, tn), lambda i, j: (0, j)),
                  pl.BlockSpec((tm, tn), lambda i, j: (i, j))],
        out_specs=pl.BlockSpec((tm, tn), lambda i, j: (i, j)),
        compiler_params=_params("parallel", "parallel"),
        name="out_proj",
    )(ya, yb, w, x)


RUNNING_MAX_INIT = -1e30
LOG2_E = 1.4426950408889634


def _qk(q, k):
    return lax.dot_general(q, k, (((1,), (1,)), ((), ())), preferred_element_type=F32)


def _pair_tables(n_tiles):
    qi = [i for i in range(n_tiles) for _ in range(i + 1)]
    kj = [j for i in range(n_tiles) for j in range(i + 1)]
    return jnp.asarray(qi, jnp.int32), jnp.asarray(kj, jnp.int32)


def _fill_vext(vext_ref, v_ref):
    vext_ref[:, :HEAD_DIM] = v_ref[...]
    vext_ref[:, HEAD_DIM:] = jnp.ones((v_ref.shape[0], HEAD_DIM), vext_ref.dtype)


def _fill_tile_bias(bias_ref, col_bias_scale, t):
    row = lax.broadcasted_iota(jnp.int32, (t, t), 0)
    col = lax.broadcasted_iota(jnp.int32, (t, t), 1)
    cb = col_bias_scale * col.astype(F32)
    bias_ref[0] = cb
    bias_ref[1] = jnp.where(col <= row, cb, -jnp.inf)


def _attn_scratch(s_len, t, n_comp):
    return [pltpu.VMEM((s_len, 2 * HEAD_DIM), BF16),
            pltpu.VMEM((2, t, t), F32),
            pltpu.VMEM((n_comp, t, t), F32),
            pltpu.VMEM((n_comp, t, t), F32),
            pltpu.VMEM((n_comp, t, t), BF16),
            pltpu.VMEM((n_comp, t, t), BF16),
            pltpu.VMEM((n_comp, t, 1), F32),
            pltpu.VMEM((n_comp, t, 1), F32),
            pltpu.VMEM((n_comp, t, 1), F32),
            pltpu.VMEM((n_comp, t, 2 * HEAD_DIM), F32),
            pltpu.VMEM((s_len // t + 1, n_comp, t, 2 * HEAD_DIM), F32)]


def _attn_pipeline(qi_ref, kj_ref, n_pairs, t, n_comp, scores, shift_step, rows_can_be_empty,
                   vext_ref, bufs, finalize):
    s_bufs, p_bufs, a_bufs, m_ref, acc_ref, done_ref = bufs
    for buf in s_bufs + p_bufs:
        buf[...] = jnp.zeros(buf.shape, buf.dtype)
    for buf in a_bufs:
        buf[...] = jnp.ones(buf.shape, buf.dtype)
    m_ref[...] = jnp.full(m_ref.shape, RUNNING_MAX_INIT, F32)
    acc_ref[...] = jnp.zeros(acc_ref.shape, F32)

    def pair(n):
        nc = jnp.clip(n, 0, n_pairs - 1)
        return qi_ref[nc], kj_ref[nc], (n >= 0) & (n < n_pairs)

    def stage_scores(n, s_buf):
        i, j, _ = pair(n)
        for c in range(n_comp):
            s_buf[c] = scores(i, j, c)

    def stage_softmax(n, s_buf, p_buf, a_buf):
        _, j, _ = pair(n)
        first = j == 0
        max_cap = jnp.where(first, RUNNING_MAX_INIT, jnp.inf)
        keep = jnp.where(first, 0.0, 1.0)
        for c in range(n_comp):
            m_prev = jnp.minimum(m_ref[c], max_cap)
            m_cur = jnp.maximum(m_prev, jnp.max(s_buf[c], axis=-1, keepdims=True))
            m_ref[c] = m_cur - shift_step
            p_buf[c] = jnp.exp2(s_buf[c] - m_cur).astype(p_buf.dtype)
            rescale = jnp.exp2(m_prev - m_cur)
            a_buf[c] = rescale * keep if rows_can_be_empty else rescale

    n_tiles = done_ref.shape[0] - 1

    def stage_values(n, p_buf, a_buf):
        i, j, active = pair(n)
        vext = vext_ref[pl.ds(pl.multiple_of(j * t, t), t), :]
        slot = jnp.where(active, i, n_tiles)
        for c in range(n_comp):
            acc = a_buf[c] * acc_ref[c] + jnp.dot(p_buf[c], vext, preferred_element_type=F32)
            acc_ref[c] = acc
            done_ref[slot, c] = acc

    def body(step, carry):
        for parity in range(2):
            n = 2 * step + parity
            stage_values(n - 2, p_bufs[parity], a_bufs[parity])
            stage_scores(n, s_bufs[parity])
            stage_softmax(n - 1, s_bufs[1 - parity], p_bufs[1 - parity], a_bufs[1 - parity])
        return carry

    lax.fori_loop(0, (n_pairs + 3) // 2, body, 0)
    for i in range(n_tiles):
        finalize(i, done_ref[i])


def _split_bufs(scratch):
    s0, s1, p0, p1, a0, a1, m_ref, acc_ref, done_ref = scratch
    return (s0, s1), (p0, p1), (a0, a1), m_ref, acc_ref, done_ref


def _diff_attn_kernel(slopes_ref, qi_ref, kj_ref, lam_ref, subw_ref, q_ref, k_ref, v_ref, o_ref,
                      qc_ref, vext_ref, bias_ref, *scratch, t, n_pairs, lambda_init):
    slope = slopes_ref[pl.program_id(1)]
    dq = DIFF_QK_DIM
    _fill_vext(vext_ref, v_ref)
    _fill_tile_bias(bias_ref, LOG2_E * slope, t)

    q_all = q_ref[...]
    lane = lax.broadcasted_iota(jnp.int32, q_all.shape, 1)
    qc_ref[0] = jnp.where(lane < dq, q_all, jnp.zeros_like(q_all))
    qc_ref[1] = jnp.where(lane >= dq, q_all, jnp.zeros_like(q_all))

    def scores(i, j, c):
        q = qc_ref[c, pl.ds(pl.multiple_of(i * t, t), t), :]
        k = k_ref[pl.ds(pl.multiple_of(j * t, t), t), :]
        return _qk(q, k) + bias_ref[(i == j).astype(jnp.int32)]

    bufs = _split_bufs(scratch)

    def finalize(i, acc):
        lam_v = lam_ref[...]
        lam = (jnp.exp(jnp.sum(lam_v[0:1, :] * lam_v[1:2, :], axis=-1, keepdims=True))
               - jnp.exp(jnp.sum(lam_v[2:3, :] * lam_v[3:4, :], axis=-1, keepdims=True))
               + lambda_init)
        a1 = acc[0]
        a2 = acc[1]
        o = (a1[:, :HEAD_DIM] / a1[:, HEAD_DIM:HEAD_DIM + 1]
             - lam * (a2[:, :HEAD_DIM] / a2[:, HEAD_DIM:HEAD_DIM + 1]))
        ms = jnp.mean(o * o, axis=-1, keepdims=True)
        o = (o * lax.rsqrt(ms + NORM_EPS)) * subw_ref[...]
        o_ref[i * t:(i + 1) * t, :] = (o * (1.0 - lambda_init)).astype(o_ref.dtype)

    _attn_pipeline(qi_ref, kj_ref, n_pairs, t, 2, scores, LOG2_E * slope * t, False,
                   vext_ref, bufs, finalize)


def _head_spec(s, blk):
    return pl.BlockSpec((None, s, HEAD_DIM), lambda bi, h: (bi, 0, blk + h))


def _diff_attention(proj3, slopes, lam4, subw, n_heads, q_blk, k_blk, v_blk, t, lambda_init):
    b, s, _ = proj3.shape
    qi, kj = _pair_tables(s // t)
    kern = functools.partial(_diff_attn_kernel, t=t, n_pairs=qi.shape[0], lambda_init=lambda_init)
    smem = pl.BlockSpec(memory_space=pltpu.SMEM)
    return pl.pallas_call(
        kern,
        out_shape=jax.ShapeDtypeStruct((b, s, n_heads * HEAD_DIM), BF16),
        grid=(b, n_heads),
        in_specs=[smem, smem, smem,
                  pl.BlockSpec((4, DIFF_QK_DIM), lambda bi, h: (0, 0)),
                  pl.BlockSpec((1, HEAD_DIM), lambda bi, h: (0, 0)),
                  _head_spec(s, q_blk), _head_spec(s, k_blk), _head_spec(s, v_blk)],
        out_specs=_head_spec(s, 0),
        scratch_shapes=[pltpu.VMEM((2, s, HEAD_DIM), BF16)] + _attn_scratch(s, t, 2),
        compiler_params=_params("parallel", "parallel"),
        name="diff_attention",
    )(slopes, qi, kj, lam4, subw, proj3, proj3, proj3)


MASKED_SCORE = -(2.0 ** 127)


def _moba_kernel(slopes_ref, qi_ref, kj_ref, q_ref, k_ref, v_ref, o_ref,
                 kmean_ref, qaug_ref, kaug_ref, vext_ref, bias_ref, *scratch,
                 t, n_pairs, n_blocks):
    slope = slopes_ref[pl.program_id(1)]
    blk = MOBA_BLOCK
    blk_shift = blk.bit_length() - 1
    s_len = q_ref.shape[0]

    _fill_vext(vext_ref, v_ref)
    _fill_tile_bias(bias_ref, LOG2_E * slope, t)

    kmean_ref[...] = jnp.zeros(kmean_ref.shape, F32)
    for n in range(n_blocks):
        kb = k_ref[n * blk:(n + 1) * blk, :].astype(F32)
        kmean_ref[n:n + 1, :] = jnp.mean(kb, axis=0, keepdims=True)
    km = kmean_ref[...]
    km_hi = km.astype(BF16)
    km_lo = (km - km_hi.astype(F32)).astype(BF16)

    nb_pad = -(-n_blocks // SUBLANES) * SUBLANES
    blk_row = lax.broadcasted_iota(jnp.int32, (nb_pad, t), 0)
    blk_rowf = blk_row.astype(F32)
    q_pos = lax.broadcasted_iota(jnp.int32, (1, t), 1)
    lane = lax.broadcasted_iota(jnp.int32, (t, HEAD_DIM), 1)
    row = lax.broadcasted_iota(jnp.int32, (t, 1), 0)
    for it in range(s_len // t):
        rows = slice(it * t, (it + 1) * t)
        q = q_ref[rows, :]
        k = k_ref[rows, :]
        own = (it * t + q_pos) >> blk_shift
        gate = (_qk(km_hi, q) + _qk(km_lo, q))[:nb_pad, :]
        neg = jnp.full_like(gate, -jnp.inf)
        g = jnp.where(blk_row < own, gate, neg)
        visible = blk_row == own
        for _ in range(min(MOBA_TOPK, n_blocks)):
            mx = jnp.max(g, axis=0, keepdims=True)
            first = jnp.min(jnp.where(g == mx, blk_rowf, float(nb_pad)), axis=0, keepdims=True)
            pick = (blk_rowf == first) & (mx > -jnp.inf)
            visible = visible | pick
            g = jnp.where(pick, neg, g)
        bias_t = jnp.where(visible | (blk_row >= n_blocks), 0.0, MASKED_SCORE)
        bias_t = jnp.concatenate([bias_t, jnp.zeros((HEAD_DIM - nb_pad, t), F32)], axis=0)
        qaug_ref[rows, :] = jnp.concatenate([q, bias_t.T.astype(BF16)], axis=1)
        own_col = (it * t + row) >> blk_shift
        kaug_ref[rows, :] = jnp.concatenate([k, (lane == own_col).astype(BF16)], axis=1)

    def scores(i, j, c):
        q = qaug_ref[pl.ds(pl.multiple_of(i * t, t), t), :]
        k = kaug_ref[pl.ds(pl.multiple_of(j * t, t), t), :]
        return _qk(q, k) + bias_ref[(i == j).astype(jnp.int32)]

    bufs = _split_bufs(scratch)

    def finalize(i, acc):
        acc = acc[0]
        o_ref[i * t:(i + 1) * t, :] = (
            acc[:, :HEAD_DIM] / acc[:, HEAD_DIM:HEAD_DIM + 1]).astype(o_ref.dtype)

    _attn_pipeline(qi_ref, kj_ref, n_pairs, t, 1, scores, LOG2_E * slope * t, True,
                   vext_ref, bufs, finalize)


def _moba_attention(proj3, slopes, n_heads, q_blk, k_blk, v_blk, t):
    b, s, _ = proj3.shape
    n_blocks = s // MOBA_BLOCK
    assert n_blocks <= HEAD_DIM, "one lane per key block"
    qi, kj = _pair_tables(s // t)
    kern = functools.partial(_moba_kernel, t=t, n_pairs=qi.shape[0], n_blocks=n_blocks)
    smem = pl.BlockSpec(memory_space=pltpu.SMEM)
    return pl.pallas_call(
        kern,
        out_shape=jax.ShapeDtypeStruct((b, s, n_heads * HEAD_DIM), BF16),
        grid=(b, n_heads),
        in_specs=[smem, smem, smem,
                  _head_spec(s, q_blk), _head_spec(s, k_blk), _head_spec(s, v_blk)],
        out_specs=_head_spec(s, 0),
        scratch_shapes=[pltpu.VMEM((HEAD_DIM, HEAD_DIM), F32),
                        pltpu.VMEM((s, 2 * HEAD_DIM), BF16),
                        pltpu.VMEM((s, 2 * HEAD_DIM), BF16)] + _attn_scratch(s, t, 1),
        compiler_params=_params("parallel", "parallel"),
        name="moba_attention",
    )(slopes, qi, kj, proj3, proj3, proj3)


def _pack_cols(a, b):
    ua = lax.bitcast_convert_type(a, jnp.uint32)
    ub = lax.bitcast_convert_type(b, jnp.uint32)
    return (ua >> 16) | (ub & jnp.uint32(0xFFFF0000))


def _unpack_cols(w):
    lo = lax.bitcast_convert_type(w << 16, F32)
    hi = lax.bitcast_convert_type(w & jnp.uint32(0xFFFF0000), F32)
    return lo.astype(BF16), hi.astype(BF16)


def _lanes_from_columns(cols, dtype):
    tm = cols[0].shape[0]
    lane = lax.broadcasted_iota(jnp.int32, (tm, len(cols)), 1)
    out = jnp.zeros((tm, len(cols)), dtype)
    for r, c in enumerate(cols):
        out = jnp.where(lane == r, c.astype(dtype), out)
    return out


def _router_kernel(h_ref, w_ref, wr_ref, br_ref, xp_ref, ids_ref, gates_ref):
    x = h_ref[...]
    ms = jnp.mean(x * x, axis=-1, keepdims=True)
    n2 = (x * lax.rsqrt(ms + NORM_EPS)) * w_ref[...]
    hi = n2.astype(BF16)
    hi_f = hi.astype(F32)
    half = n2.shape[1] // 2
    xp_ref[...] = _pack_cols(hi_f[:, :half], hi_f[:, half:])

    lo = (n2 - hi_f).astype(BF16)
    wr = wr_ref[...]
    wr_hi = wr.astype(BF16)
    wr_lo = (wr - wr_hi.astype(F32)).astype(BF16)
    logits = (jnp.dot(hi, wr_hi, preferred_element_type=F32)
              + jnp.dot(lo, wr_hi, preferred_element_type=F32)
              + jnp.dot(hi, wr_lo, preferred_element_type=F32)) + br_ref[...]

    n_exp = logits.shape[1]
    eid = lax.broadcasted_iota(jnp.int32, logits.shape, 1).astype(F32)
    neg = jnp.full_like(logits, -jnp.inf)
    g = logits
    vals, idxs = [], []
    for _ in range(TOP_K):
        mx = jnp.max(g, axis=-1, keepdims=True)
        first = jnp.min(jnp.where(g == mx, eid, float(n_exp)), axis=-1, keepdims=True)
        vals.append(mx)
        idxs.append(first)
        g = jnp.where(eid == first, neg, g)
    exps = [jnp.exp(v - vals[0]) for v in vals]
    denom = exps[0] + exps[1] + exps[2] + exps[3]
    ids_ref[...] = _lanes_from_columns(idxs, jnp.int32)
    gates_ref[...] = _lanes_from_columns([e / denom for e in exps], F32)


def _router(h, norm_w, w_router, b_router, tm):
    t, d = h.shape
    n_exp = w_router.shape[1]
    return pl.pallas_call(
        _router_kernel,
        out_shape=(jax.ShapeDtypeStruct((t, d // 2), jnp.uint32),
                   jax.ShapeDtypeStruct((t, TOP_K), jnp.int32),
                   jax.ShapeDtypeStruct((t, TOP_K), F32)),
        grid=(t // tm,),
        in_specs=[pl.BlockSpec((tm, d), lambda i: (i, 0)),
                  pl.BlockSpec((1, d), lambda i: (0, 0)),
                  pl.BlockSpec((d, n_exp), lambda i: (0, 0)),
                  pl.BlockSpec((1, n_exp), lambda i: (0, 0))],
        out_specs=(pl.BlockSpec((tm, d // 2), lambda i: (i, 0)),
                   pl.BlockSpec((tm, TOP_K), lambda i: (i, 0)),
                   pl.BlockSpec((tm, TOP_K), lambda i: (i, 0))),
        compiler_params=_params("parallel"),
        name="rmsnorm2_router",
    )(h, norm_w.reshape(1, d), w_router, b_router.reshape(1, n_exp))


def _rank_kernel(ids_ref, rank_ref, counts_ref, carry_ref, *, n_exp):
    i = pl.program_id(0)

    @pl.when(i == 0)
    def _():
        carry_ref[...] = jnp.zeros_like(carry_ref)

    ids = ids_ref[...]
    tm = ids.shape[0]
    eid = lax.broadcasted_iota(jnp.int32, (tm, n_exp), 1)
    onehots = [(eid == ids[:, r:r + 1]).astype(F32) for r in range(TOP_K)]
    total = onehots[0] + onehots[1] + onehots[2] + onehots[3]
    row = lax.broadcasted_iota(jnp.int32, (tm, tm), 0)
    col = lax.broadcasted_iota(jnp.int32, (tm, tm), 1)
    strict_lower = (col < row).astype(BF16)
    before = jnp.dot(strict_lower, total.astype(BF16), preferred_element_type=F32)
    before = before + carry_ref[...]
    ranks = [jnp.sum(oh * before, axis=-1, keepdims=True) for oh in onehots]
    rank_ref[...] = _lanes_from_columns(ranks, jnp.int32)
    carry_ref[...] += jnp.sum(total, axis=0, keepdims=True)
    counts_ref[...] = carry_ref[...]


def _rank(ids, n_exp, tm):
    t = ids.shape[0]
    return pl.pallas_call(
        functools.partial(_rank_kernel, n_exp=n_exp),
        out_shape=(jax.ShapeDtypeStruct((t, TOP_K), jnp.int32),
                   jax.ShapeDtypeStruct((1, n_exp), F32)),
        grid=(t // tm,),
        in_specs=[pl.BlockSpec((tm, TOP_K), lambda i: (i, 0))],
        out_specs=(pl.BlockSpec((tm, TOP_K), lambda i: (i, 0)),
                   pl.BlockSpec((1, n_exp), lambda i: (0, 0))),
        scratch_shapes=[pltpu.VMEM((1, n_exp), F32)],
        compiler_params=_params("arbitrary"),
        name="expert_rank",
    )(ids)


def _dispatch_kernel(fill_start_ref, fill_n_ref, pos_ref, x_ref, buf_ref,
                     zeros_ref, sem, zsem, *, n_exp, tile_rows):
    i = pl.program_id(0)
    tm = x_ref.shape[0]
    zrows = zeros_ref.shape[0]
    sub = SUBLANES
    n_bits = (zrows // sub - 1).bit_length()

    def head_copy(e, r):
        a = fill_start_ref[e]
        n_head = jnp.minimum((-a) & (sub - 1), fill_n_ref[e])
        return r < n_head, pltpu.make_async_copy(
            zeros_ref.at[pl.ds(0, 1), :], buf_ref.at[pl.ds(a + r, 1), :], zsem)

    def body_copy(e, bit):
        a = fill_start_ref[e]
        n_head = jnp.minimum((-a) & (sub - 1), fill_n_ref[e])
        groups = (fill_n_ref[e] - n_head) // sub
        size = sub << bit
        dst = pl.multiple_of(a + n_head + sub * (groups & ((1 << bit) - 1)), sub)
        return (groups & (1 << bit)) != 0, pltpu.make_async_copy(
            zeros_ref.at[pl.ds(0, size), :], buf_ref.at[pl.ds(dst, size), :], zsem)

    def fill_copies():
        for e in range(n_exp):
            for r in range(sub - 1):
                yield head_copy(e, r)
            for bit in range(n_bits):
                yield body_copy(e, bit)

    def tail_copy(c):
        dst = pl.multiple_of(c * zrows, zrows)
        return pltpu.make_async_copy(zeros_ref, buf_ref.at[pl.ds(dst, zrows), :], zsem)

    @pl.when(i == 0)
    def _():
        zeros_ref[...] = jnp.zeros(zeros_ref.shape, zeros_ref.dtype)
        first_tail = fill_start_ref[n_exp] // zrows
        n_chunks = buf_ref.shape[0] // zrows

        def tail_start(c, carry):
            tail_copy(c).start()
            return carry

        def tail_wait(c, carry):
            tail_copy(c).wait()
            return carry

        for live, cp in fill_copies():
            pl.when(live)(cp.start)
        lax.fori_loop(first_tail, n_chunks, tail_start, 0)
        for live, cp in fill_copies():
            pl.when(live)(cp.wait)
        lax.fori_loop(first_tail, n_chunks, tail_wait, 0)

    def start(tok, carry):
        src = x_ref.at[pl.ds(tok, 1), :]
        for slot in range(TOP_K):
            dst = buf_ref.at[pl.ds(pos_ref[tok * TOP_K + slot], 1), :]
            pltpu.make_async_copy(src, dst, sem).start()
        return carry

    lax.fori_loop(0, tm, start, 0, unroll=DMA_ISSUE_UNROLL)
    for _ in range(TOP_K):
        pltpu.make_async_copy(x_ref, buf_ref.at[pl.ds(0, tm), :], sem).wait()


def _dispatch(fill_start, fill_n, pos_flat, xp, n_rows, tm, tile_rows):
    t, w = xp.shape
    n_exp = fill_n.shape[0]
    zrows = tile_rows
    grid_spec = pltpu.PrefetchScalarGridSpec(
        num_scalar_prefetch=2,
        grid=(t // tm,),
        in_specs=[pl.BlockSpec((tm * TOP_K,), lambda i, fs, fn: (i,), memory_space=pltpu.SMEM),
                  pl.BlockSpec((tm, w), lambda i, fs, fn: (i, 0))],
        out_specs=pl.BlockSpec(memory_space=pl.ANY),
        scratch_shapes=[pltpu.VMEM((zrows, w), xp.dtype),
                        pltpu.SemaphoreType.DMA(()),
                        pltpu.SemaphoreType.DMA(())],
    )
    return pl.pallas_call(
        functools.partial(_dispatch_kernel, n_exp=n_exp, tile_rows=tile_rows),
        out_shape=jax.ShapeDtypeStruct((n_rows, w), xp.dtype),
        grid_spec=grid_spec,
        compiler_params=_params("arbitrary"),
        name="dispatch",
    )(fill_start, fill_n, pos_flat, xp)


def _deinterleave_matrix(n):
    r = lax.broadcasted_iota(jnp.int32, (n, n), 0)
    c = lax.broadcasted_iota(jnp.int32, (n, n), 1)
    src = jnp.where(c < n // 2, 2 * c, 2 * (c - n // 2) + 1)
    return (r == src).astype(BF16)


def _gateup_kernel(te_ref, tv_ref, tf_ref, tr_ref, x_ref, w_ref, bg_ref, bu_ref, o_ref,
                   wg_ref, wu_ref):
    del te_ref, tr_ref
    i = pl.program_id(1)
    grp = V7X_MXU_DIM
    half = grp // 2

    @pl.when(tf_ref[i] == 1)
    def _():
        perm = _deinterleave_matrix(grp)
        for g in range(w_ref.shape[1] // grp):
            w = w_ref[:, g * grp:(g + 1) * grp].astype(BF16)
            wp = jnp.dot(w, perm, preferred_element_type=F32).astype(BF16)
            wg_ref[:, g * half:(g + 1) * half] = wp[:, :half]
            wu_ref[:, g * half:(g + 1) * half] = wp[:, half:]

    @pl.when(tv_ref[i] == 1)
    def _():
        xa, xb = _unpack_cols(x_ref[...])
        k_half = xa.shape[1]

        def proj(w_s, b_ref):
            return (jnp.dot(xa, w_s[:k_half, :], preferred_element_type=F32)
                    + jnp.dot(xb, w_s[k_half:, :], preferred_element_type=F32)
                    + b_ref[...])

        g = jnp.minimum(proj(wg_ref, bg_ref), SWIGLU_LIMIT)
        u = jnp.clip(proj(wu_ref, bu_ref), -SWIGLU_LIMIT, SWIGLU_LIMIT)
        sig = 1.0 / (1.0 + jnp.exp(-(g * SWIGLU_ALPHA)))
        o_ref[...] = ((u + 1.0) * (g * sig)).astype(o_ref.dtype)

    @pl.when(tv_ref[i] == 0)
    def _():
        o_ref[...] = jnp.zeros_like(o_ref)


def _gateup(tile_expert, tile_valid, tile_first, tile_row, xs, w_gu, bg, bu, tm, fc):
    rows, k_half = xs.shape
    n_exp, d, f2 = w_gu.shape
    f = f2 // 2
    n_tiles = rows // tm
    n_chunks = f // fc
    last_chunk = n_chunks - 1

    def w_map(c, i, te, tv, tf, tr):
        return (te[i], 0, jnp.where(tv[i] == 1, c, last_chunk))

    grid_spec = pltpu.PrefetchScalarGridSpec(
        num_scalar_prefetch=4,
        grid=(n_chunks, n_tiles),
        in_specs=[pl.BlockSpec((tm, k_half), lambda c, i, te, tv, tf, tr: (tr[i], 0)),
                  pl.BlockSpec((None, d, 2 * fc), w_map),
                  pl.BlockSpec((None, 1, fc), w_map),
                  pl.BlockSpec((None, 1, fc), w_map)],
        out_specs=pl.BlockSpec((tm, fc), lambda c, i, te, tv, tf, tr: (i, c)),
        scratch_shapes=[pltpu.VMEM((d, fc), BF16), pltpu.VMEM((d, fc), BF16)],
    )
    return pl.pallas_call(
        _gateup_kernel,
        out_shape=jax.ShapeDtypeStruct((rows, f), BF16),
        grid_spec=grid_spec,
        compiler_params=_params("arbitrary", "arbitrary"),
        name="expert_gate_up",
    )(tile_expert, tile_valid, tile_first, tile_row, xs, w_gu, bg, bu)


def _down_kernel(te_ref, tv_ref, tf_ref, a_ref, w_ref, b_ref, o_ref, wb_ref):
    i = pl.program_id(1)

    @pl.when(tf_ref[i] == 1)
    def _():
        wb_ref[...] = w_ref[...].astype(BF16)

    @pl.when(tv_ref[i] == 1)
    def _():
        o_ref[...] = jnp.dot(a_ref[...], wb_ref[...], preferred_element_type=F32) + b_ref[...]

    @pl.when(tv_ref[i] == 0)
    def _():
        o_ref[...] = jnp.zeros_like(o_ref)


def _down(tile_expert, tile_valid, tile_first, act, wd, bd, tm, tn):
    rows, f = act.shape
    n_exp, _, d = wd.shape
    n_chunks = d // tn
    last_chunk = n_chunks - 1

    def w_map(c, i, te, tv, tf):
        return (te[i], 0, jnp.where(tv[i] == 1, c, last_chunk))

    grid_spec = pltpu.PrefetchScalarGridSpec(
        num_scalar_prefetch=3,
        grid=(n_chunks, rows // tm),
        in_specs=[pl.BlockSpec((tm, f), lambda c, i, te, tv, tf: (i, 0)),
                  pl.BlockSpec((None, f, tn), w_map),
                  pl.BlockSpec((None, 1, tn), w_map)],
        out_specs=pl.BlockSpec((tm, tn), lambda c, i, te, tv, tf: (i, c)),
        scratch_shapes=[pltpu.VMEM((f, tn), BF16)],
    )
    return pl.pallas_call(
        _down_kernel,
        out_shape=jax.ShapeDtypeStruct((rows, d), F32),
        grid_spec=grid_spec,
        compiler_params=_params("arbitrary", "arbitrary"),
        name="expert_down",
    )(tile_expert, tile_valid, tile_first, act, wd, bd)


def _combine_kernel(pos_ref, gates_ref, h_ref, w_ref, y_ref, o_ref, rows_ref, sem):
    tm = h_ref.shape[0]

    def start(tok, carry):
        for slot in range(TOP_K):
            src = y_ref.at[pl.ds(pos_ref[tok * TOP_K + slot], 1), :]
            pltpu.make_async_copy(src, rows_ref.at[slot, pl.ds(tok, 1), :], sem).start()
        return carry

    lax.fori_loop(0, tm, start, 0, unroll=DMA_ISSUE_UNROLL)
    for r in range(TOP_K):
        pltpu.make_async_copy(y_ref.at[pl.ds(0, tm), :], rows_ref.at[r], sem).wait()

    gates = gates_ref[...]
    moe = gates[:, 0:1] * rows_ref[0]
    for r in range(1, TOP_K):
        moe += gates[:, r:r + 1] * rows_ref[r]
    x = h_ref[...] + moe
    ms = jnp.mean(x * x, axis=-1, keepdims=True)
    o_ref[...] = (x * lax.rsqrt(ms + NORM_EPS)) * w_ref[...]


def _combine(pos_flat, gates, h, norm_w, ys, tm):
    t, d = h.shape
    return pl.pallas_call(
        _combine_kernel,
        out_shape=jax.ShapeDtypeStruct((t, d), F32),
        grid=(t // tm,),
        in_specs=[pl.BlockSpec((tm * TOP_K,), lambda i: (i,), memory_space=pltpu.SMEM),
                  pl.BlockSpec((tm, TOP_K), lambda i: (i, 0)),
                  pl.BlockSpec((tm, d), lambda i: (i, 0)),
                  pl.BlockSpec((1, d), lambda i: (0, 0)),
                  pl.BlockSpec(memory_space=pl.ANY)],
        out_specs=pl.BlockSpec((tm, d), lambda i: (i, 0)),
        scratch_shapes=[pltpu.VMEM((TOP_K, tm, d), F32),
                        pltpu.SemaphoreType.DMA(())],
        compiler_params=_params("arbitrary"),
        name="combine_final_norm",
    )(pos_flat, gates, h, norm_w.reshape(1, d), ys)


def _tiles(t, s, d, f):
    return dict(
        norm_rows=min(256, t),
        mm_rows=min(1024, t),
        mm_cols=min(1024, d),
        attn_tile=min(512, s),
        router_rows=min(256, t),
        rank_rows=min(512, t),
        dispatch_rows=min(256, t),
        expert_rows=min(512, t),
        ff_chunk=min(256, f),
        down_cols=min(2048, d),
        combine_rows=min(256, t),
    )


def _alibi_slopes(n):
    return jnp.exp2(-ALIBI_MAX_BIAS * jnp.arange(1, n + 1, dtype=F32) / n)


def _routing_tables(ids, rank, counts, tm, n_tiles):
    n_exp = counts.shape[0]
    i32 = jnp.int32
    padded = ((counts + tm - 1) // tm) * tm
    ends = jnp.cumsum(padded)
    starts = ends - padded
    onehot = ids[..., None] == jnp.arange(n_exp, dtype=i32)
    pos = rank + jnp.sum(jnp.where(onehot, starts, 0), axis=-1)
    tile_start = jnp.arange(n_tiles, dtype=i32) * tm
    tile_valid = (tile_start < ends[-1]).astype(i32)
    tile_expert = jnp.minimum(jnp.sum((tile_start[:, None] >= ends[None, :]).astype(i32), axis=1),
                              n_exp - 1)
    prev_expert = jnp.concatenate([jnp.full((1,), -1, i32), tile_expert[:-1]])
    tile_first = tile_valid * (tile_expert != prev_expert).astype(i32)
    tile_row = jnp.minimum(jnp.arange(n_tiles, dtype=i32), ends[-1] // tm - 1)
    fill_start = jnp.concatenate([starts + counts, ends[-1:]]).astype(i32)
    fill_n = (padded - counts).astype(i32)
    return pos.reshape(-1), tile_expert, tile_valid, tile_first, tile_row, fill_start, fill_n


def _layer(h, l, norm1_w, w_in, lam_q1, lam_k1, lam_q2, lam_k2, subln_w, w_out, norm2_w,
           w_router, b_router, w_gate_up, b_gate_up, w_down, b_down, out_norm_w):
    b, s, d = h.shape
    t = b * s
    n_exp = w_router.shape[-1]
    f = w_down.shape[-2]
    n_heads = d // 2 // HEAD_DIM
    width = n_heads * HEAD_DIM
    tl = _tiles(t, s, d, f)
    lambda_init = 0.8 - 0.6 * math.exp(-0.3 * l)
    x2 = h.reshape(t, d)

    n1 = _rmsnorm(x2, norm1_w[l], tl["norm_rows"])
    n_cols = w_in.shape[-1]
    col_scale = jnp.ones((n_cols,), F32)
    col_scale = col_scale.at[:width].set(LOG2_E * DIFF_QK_DIM ** -0.5)
    col_scale = col_scale.at[3 * width:4 * width].set(LOG2_E * HEAD_DIM ** -0.5)
    proj = _matmul(n1, w_in[l].astype(BF16), col_scale.reshape(1, n_cols), tl["mm_rows"],
                   tl["mm_cols"], BF16)
    proj3 = proj.reshape(b, s, proj.shape[1])
    slopes = _alibi_slopes(n_heads)
    lam4 = jnp.stack([lam_q1[l], lam_k1[l], lam_q2[l], lam_k2[l]]).astype(F32)
    blocks = width // HEAD_DIM
    y_a = _diff_attention(proj3, slopes, lam4, subln_w[l].reshape(1, HEAD_DIM).astype(F32),
                          n_heads, 0, blocks, 2 * blocks, tl["attn_tile"], lambda_init)
    y_b = _moba_attention(proj3, slopes, n_heads, 3 * blocks, 4 * blocks, 5 * blocks,
                          tl["attn_tile"])
    h1 = _outproj(y_a.reshape(t, width), y_b.reshape(t, width), w_out[l].astype(BF16), x2,
                  tl["mm_rows"], tl["mm_cols"])

    xp, ids, gates = _router(h1, norm2_w[l], w_router[l], b_router[l], tl["router_rows"])
    rank, counts = _rank(ids, n_exp, tl["rank_rows"])
    tm = tl["expert_rows"]
    n_tiles = (t * TOP_K) // tm + n_exp
    (pos_flat, tile_expert, tile_valid, tile_first, tile_row, fill_start, fill_n) = \
        _routing_tables(ids, rank, counts.reshape(n_exp).astype(jnp.int32), tm, n_tiles)

    xs = _dispatch(fill_start, fill_n, pos_flat, xp, n_tiles * tm, tl["dispatch_rows"], tm)
    bgu = b_gate_up[l].reshape(n_exp, 1, f, 2)
    act = _gateup(tile_expert, tile_valid, tile_first, tile_row, xs, w_gate_up[l],
                  bgu[..., 0], bgu[..., 1], tm, tl["ff_chunk"])
    ys = _down(tile_expert, tile_valid, tile_first, act, w_down[l],
               b_down[l].reshape(n_exp, 1, d), tm, tl["down_cols"])
    return _combine(pos_flat, gates, h1, out_norm_w, ys, tl["combine_rows"]).reshape(b, s, d)


def kernel(x, norm1_w, w_in, lam_q1, lam_k1, lam_q2, lam_k2, subln_w, w_out, norm2_w,
           w_router, b_router, w_gate_up, b_gate_up, w_down, b_down, final_norm_w):
    depth = w_in.shape[0]
    assert depth == 1, "the fused combine + final norm stage assumes a single layer"
    return _layer(x, 0, norm1_w, w_in, lam_q1, lam_k1, lam_q2, lam_k2, subln_w, w_out,
                  norm2_w, w_router, b_router, w_gate_up, b_gate_up, w_down, b_down,
                  final_norm_w)
```

```python
import functools
import math

import jax
import jax.numpy as jnp
from jax import lax
from jax.experimental import pallas as pl
from jax.experimental.pallas import tpu as pltpu

F32 = jnp.float32
BF16 = jnp.bfloat16

HEAD_DIM = 128
DIFF_QK_DIM = HEAD_DIM // 2
MOBA_BLOCK = 256
MOBA_TOPK = 3
TOP_K = 4
SWIGLU_LIMIT = 7.0
SWIGLU_ALPHA = 1.702
NORM_EPS = 1e-5
ALIBI_MAX_BIAS = 8.0

V7X_VMEM_BYTES = 64 * 1024 * 1024
VMEM_LIMIT_BYTES = V7X_VMEM_BYTES - 8 * 1024 * 1024
V7X_MXU_DIM = 256
SUBLANES = 8
DMA_ISSUE_UNROLL = 8


def _params(*semantics):
    return pltpu.CompilerParams(dimension_semantics=semantics,
                                vmem_limit_bytes=VMEM_LIMIT_BYTES)


def _rmsnorm_kernel(x_ref, w_ref, o_ref):
    x = x_ref[...]
    ms = jnp.mean(x * x, axis=-1, keepdims=True)
    o_ref[...] = ((x * lax.rsqrt(ms + NORM_EPS)) * w_ref[...]).astype(o_ref.dtype)


def _rmsnorm(x, w, tm):
    t, d = x.shape
    return pl.pallas_call(
        _rmsnorm_kernel,
        out_shape=jax.ShapeDtypeStruct((t, d), BF16),
        grid=(t // tm,),
        in_specs=[pl.BlockSpec((tm, d), lambda i: (i, 0)),
                  pl.BlockSpec((1, d), lambda i: (0, 0))],
        out_specs=pl.BlockSpec((tm, d), lambda i: (i, 0)),
        compiler_params=_params("parallel"),
        name="rmsnorm1",
    )(x, w.reshape(1, d))


def _matmul_kernel(a_ref, b_ref, cs_ref, o_ref):
    acc = jnp.dot(a_ref[...], b_ref[...], preferred_element_type=F32)
    o_ref[...] = (acc * cs_ref[...]).astype(o_ref.dtype)


def _matmul(a, b, col_scale, tm, tn, out_dtype):
    m, k = a.shape
    _, n = b.shape
    return pl.pallas_call(
        _matmul_kernel,
        out_shape=jax.ShapeDtypeStruct((m, n), out_dtype),
        grid=(m // tm, n // tn),
        in_specs=[pl.BlockSpec((tm, k), lambda i, j: (i, 0)),
                  pl.BlockSpec((k, tn), lambda i, j: (0, j)),
                  pl.BlockSpec((1, tn), lambda i, j: (0, j))],
        out_specs=pl.BlockSpec((tm, tn), lambda i, j: (i, j)),
        compiler_params=_params("parallel", "parallel"),
        name="in_proj",
    )(a, b, col_scale)


def _outproj_kernel(ya_ref, yb_ref, w_ref, x_ref, o_ref):
    ka = ya_ref.shape[1]
    acc = jnp.dot(ya_ref[...], w_ref[:ka, :], preferred_element_type=F32)
    acc += jnp.dot(yb_ref[...], w_ref[ka:, :], preferred_element_type=F32)
    o_ref[...] = x_ref[...] + acc


def _outproj(ya, yb, w, x, tm, tn):
    m, ka = ya.shape
    kb = yb.shape[1]
    n = w.shape[1]
    return pl.pallas_call(
        _outproj_kernel,
        out_shape=jax.ShapeDtypeStruct((m, n), F32),
        grid=(m // tm, n // tn),
        in_specs=[pl.BlockSpec((tm, ka), lambda i, j: (i, 0)),
                  pl.BlockSpec((tm, kb), lambda i, j: (i, 0)),
                  pl.BlockSpec((ka + kb, tn), lambda i, j: (0, j)),
                  pl.BlockSpec((tm, tn), lambda i, j: (i, j))],
        out_specs=pl.BlockSpec((tm, tn), lambda i, j: (i, j)),
        compiler_params=_params("parallel", "parallel"),
        name="out_proj",
    )(ya, yb, w, x)


RUNNING_MAX_INIT = -1e30
LOG2_E = 1.4426950408889634


def _qk(q, k):
    return lax.dot_general(q, k, (((1,), (1,)), ((), ())), preferred_element_type=F32)


def _pair_tables(n_tiles):
    qi = [i for i in range(n_tiles) for _ in range(i + 1)]
    kj = [j for i in range(n_tiles) for j in range(i + 1)]
    return jnp.asarray(qi, jnp.int32), jnp.asarray(kj, jnp.int32)


def _fill_vext(vext_ref, v_ref):
    vext_ref[:, :HEAD_DIM] = v_ref[...]
    vext_ref[:, HEAD_DIM:] = jnp.ones((v_ref.shape[0], HEAD_DIM), vext_ref.dtype)


def _fill_tile_bias(bias_ref, col_bias_scale, t):
    row = lax.broadcasted_iota(jnp.int32, (t, t), 0)
    col = lax.broadcasted_iota(jnp.int32, (t, t), 1)
    cb = col_bias_scale * col.astype(F32)
    bias_ref[0] = cb
    bias_ref[1] = jnp.where(col <= row, cb, -jnp.inf)


def _attn_scratch(s_len, t, n_comp):
    return [pltpu.VMEM((s_len, 2 * HEAD_DIM), BF16),
            pltpu.VMEM((2, t, t), F32),
            pltpu.VMEM((n_comp, t, t), F32),
            pltpu.VMEM((n_comp, t, t), F32),
            pltpu.VMEM((n_comp, t, t), BF16),
            pltpu.VMEM((n_comp, t, t), BF16),
            pltpu.VMEM((n_comp, t, 1), F32),
            pltpu.VMEM((n_comp, t, 1), F32),
            pltpu.VMEM((n_comp, t, 1), F32),
            pltpu.VMEM((n_comp, t, 2 * HEAD_DIM), F32),
            pltpu.VMEM((s_len // t + 1, n_comp, t, 2 * HEAD_DIM), F32)]


def _attn_pipeline(qi_ref, kj_ref, n_pairs, t, n_comp, scores, shift_step, rows_can_be_empty,
                   vext_ref, bufs, finalize):
    s_bufs, p_bufs, a_bufs, m_ref, acc_ref, done_ref = bufs
    for buf in s_bufs + p_bufs:
        buf[...] = jnp.zeros(buf.shape, buf.dtype)
    for buf in a_bufs:
        buf[...] = jnp.ones(buf.shape, buf.dtype)
    m_ref[...] = jnp.full(m_ref.shape, RUNNING_MAX_INIT, F32)
    acc_ref[...] = jnp.zeros(acc_ref.shape, F32)

    def pair(n):
        nc = jnp.clip(n, 0, n_pairs - 1)
        return qi_ref[nc], kj_ref[nc], (n >= 0) & (n < n_pairs)

    def stage_scores(n, s_buf):
        i, j, _ = pair(n)
        for c in range(n_comp):
            s_buf[c] = scores(i, j, c)

    def stage_softmax(n, s_buf, p_buf, a_buf):
        _, j, _ = pair(n)
        first = j == 0
        max_cap = jnp.where(first, RUNNING_MAX_INIT, jnp.inf)
        keep = jnp.where(first, 0.0, 1.0)
        for c in range(n_comp):
            m_prev = jnp.minimum(m_ref[c], max_cap)
            m_cur = jnp.maximum(m_prev, jnp.max(s_buf[c], axis=-1, keepdims=True))
            m_ref[c] = m_cur - shift_step
            p_buf[c] = jnp.exp2(s_buf[c] - m_cur).astype(p_buf.dtype)
            rescale = jnp.exp2(m_prev - m_cur)
            a_buf[c] = rescale * keep if rows_can_be_empty else rescale

    n_tiles = done_ref.shape[0] - 1

    def stage_values(n, p_buf, a_buf):
        i, j, active = pair(n)
        vext = vext_ref[pl.ds(pl.multiple_of(j * t, t), t), :]
        slot = jnp.where(active, i, n_tiles)
        for c in range(n_comp):
            acc = a_buf[c] * acc_ref[c] + jnp.dot(p_buf[c], vext, preferred_element_type=F32)
            acc_ref[c] = acc
            done_ref[slot, c] = acc

    def body(step, carry):
        for parity in range(2):
            n = 2 * step + parity
            stage_values(n - 2, p_bufs[parity], a_bufs[parity])
            stage_scores(n, s_bufs[parity])
            stage_softmax(n - 1, s_bufs[1 - parity], p_bufs[1 - parity], a_bufs[1 - parity])
        return carry

    lax.fori_loop(0, (n_pairs + 3) // 2, body, 0)
    for i in range(n_tiles):
        finalize(i, done_ref[i])


def _split_bufs(scratch):
    s0, s1, p0, p1, a0, a1, m_ref, acc_ref, done_ref = scratch
    return (s0, s1), (p0, p1), (a0, a1), m_ref, acc_ref, done_ref


def _diff_attn_kernel(slopes_ref, qi_ref, kj_ref, lam_ref, subw_ref, q_ref, k_ref, v_ref, o_ref,
                      qc_ref, vext_ref, bias_ref, *scratch, t, n_pairs, lambda_init):
    slope = slopes_ref[pl.program_id(1)]
    dq = DIFF_QK_DIM
    _fill_vext(vext_ref, v_ref)
    _fill_tile_bias(bias_ref, LOG2_E * slope, t)

    q_all = q_ref[...]
    lane = lax.broadcasted_iota(jnp.int32, q_all.shape, 1)
    qc_ref[0] = jnp.where(lane < dq, q_all, jnp.zeros_like(q_all))
    qc_ref[1] = jnp.where(lane >= dq, q_all, jnp.zeros_like(q_all))

    def scores(i, j, c):
        q = qc_ref[c, pl.ds(pl.multiple_of(i * t, t), t), :]
        k = k_ref[pl.ds(pl.multiple_of(j * t, t), t), :]
        return _qk(q, k) + bias_ref[(i == j).astype(jnp.int32)]

    bufs = _split_bufs(scratch)

    def finalize(i, acc):
        lam_v = lam_ref[...]
        lam = (jnp.exp(jnp.sum(lam_v[0:1, :] * lam_v[1:2, :], axis=-1, keepdims=True))
               - jnp.exp(jnp.sum(lam_v[2:3, :] * lam_v[3:4, :], axis=-1, keepdims=True))
               + lambda_init)
        a1 = acc[0]
        a2 = acc[1]
        o = (a1[:, :HEAD_DIM] / a1[:, HEAD_DIM:HEAD_DIM + 1]
             - lam * (a2[:, :HEAD_DIM] / a2[:, HEAD_DIM:HEAD_DIM + 1]))
        ms = jnp.mean(o * o, axis=-1, keepdims=True)
        o = (o * lax.rsqrt(ms + NORM_EPS)) * subw_ref[...]
        o_ref[i * t:(i + 1) * t, :] = (o * (1.0 - lambda_init)).astype(o_ref.dtype)

    _attn_pipeline(qi_ref, kj_ref, n_pairs, t, 2, scores, LOG2_E * slope * t, False,
                   vext_ref, bufs, finalize)


def _head_spec(s, blk):
    return pl.BlockSpec((None, s, HEAD_DIM), lambda bi, h: (bi, 0, blk + h))


def _diff_attention(proj3, slopes, lam4, subw, n_heads, q_blk, k_blk, v_blk, t, lambda_init):
    b, s, _ = proj3.shape
    qi, kj = _pair_tables(s // t)
    kern = functools.partial(_diff_attn_kernel, t=t, n_pairs=qi.shape[0], lambda_init=lambda_init)
    smem = pl.BlockSpec(memory_space=pltpu.SMEM)
    return pl.pallas_call(
        kern,
        out_shape=jax.ShapeDtypeStruct((b, s, n_heads * HEAD_DIM), BF16),
        grid=(b, n_heads),
        in_specs=[smem, smem, smem,
                  pl.BlockSpec((4, DIFF_QK_DIM), lambda bi, h: (0, 0)),
                  pl.BlockSpec((1, HEAD_DIM), lambda bi, h: (0, 0)),
                  _head_spec(s, q_blk), _head_spec(s, k_blk), _head_spec(s, v_blk)],
        out_specs=_head_spec(s, 0),
        scratch_shapes=[pltpu.VMEM((2, s, HEAD_DIM), BF16)] + _attn_scratch(s, t, 2),
        compiler_params=_params("parallel", "parallel"),
        name="diff_attention",
    )(slopes, qi, kj, lam4, subw, proj3, proj3, proj3)


MASKED_SCORE = -(2.0 ** 127)


def _moba_kernel(slopes_ref, qi_ref, kj_ref, q_ref, k_ref, v_ref, o_ref,
                 kmean_ref, qaug_ref, kaug_ref, vext_ref, bias_ref, *scratch,
                 t, n_pairs, n_blocks):
    slope = slopes_ref[pl.program_id(1)]
    blk = MOBA_BLOCK
    blk_shift = blk.bit_length() - 1
    s_len = q_ref.shape[0]

    _fill_vext(vext_ref, v_ref)
    _fill_tile_bias(bias_ref, LOG2_E * slope, t)

    kmean_ref[...] = jnp.zeros(kmean_ref.shape, F32)
    for n in range(n_blocks):
        kb = k_ref[n * blk:(n + 1) * blk, :].astype(F32)
        kmean_ref[n:n + 1, :] = jnp.mean(kb, axis=0, keepdims=True)
    km = kmean_ref[...]
    km_hi = km.astype(BF16)
    km_lo = (km - km_hi.astype(F32)).astype(BF16)

    nb_pad = -(-n_blocks // SUBLANES) * SUBLANES
    blk_row = lax.broadcasted_iota(jnp.int32, (nb_pad, t), 0)
    blk_rowf = blk_row.astype(F32)
    q_pos = lax.broadcasted_iota(jnp.int32, (1, t), 1)
    lane = lax.broadcasted_iota(jnp.int32, (t, HEAD_DIM), 1)
    row = lax.broadcasted_iota(jnp.int32, (t, 1), 0)
    for it in range(s_len // t):
        rows = slice(it * t, (it + 1) * t)
        q = q_ref[rows, :]
        k = k_ref[rows, :]
        own = (it * t + q_pos) >> blk_shift
        gate = (_qk(km_hi, q) + _qk(km_lo, q))[:nb_pad, :]
        neg = jnp.full_like(gate, -jnp.inf)
        g = jnp.where(blk_row < own, gate, neg)
        visible = blk_row == own
        for _ in range(min(MOBA_TOPK, n_blocks)):
            mx = jnp.max(g, axis=0, keepdims=True)
            first = jnp.min(jnp.where(g == mx, blk_rowf, float(nb_pad)), axis=0, keepdims=True)
            pick = (blk_rowf == first) & (mx > -jnp.inf)
            visible = visible | pick
            g = jnp.where(pick, neg, g)
        bias_t = jnp.where(visible | (blk_row >= n_blocks), 0.0, MASKED_SCORE)
        bias_t = jnp.concatenate([bias_t, jnp.zeros((HEAD_DIM - nb_pad, t), F32)], axis=0)
        qaug_ref[rows, :] = jnp.concatenate([q, bias_t.T.astype(BF16)], axis=1)
        own_col = (it * t + row) >> blk_shift
        kaug_ref[rows, :] = jnp.concatenate([k, (lane == own_col).astype(BF16)], axis=1)

    def scores(i, j, c):
        q = qaug_ref[pl.ds(pl.multiple_of(i * t, t), t), :]
        k = kaug_ref[pl.ds(pl.multiple_of(j * t, t), t), :]
        return _qk(q, k) + bias_ref[(i == j).astype(jnp.int32)]

    bufs = _split_bufs(scratch)

    def finalize(i, acc):
        acc = acc[0]
        o_ref[i * t:(i + 1) * t, :] = (
            acc[:, :HEAD_DIM] / acc[:, HEAD_DIM:HEAD_DIM + 1]).astype(o_ref.dtype)

    _attn_pipeline(qi_ref, kj_ref, n_pairs, t, 1, scores, LOG2_E * slope * t, True,
                   vext_ref, bufs, finalize)


def _moba_attention(proj3, slopes, n_heads, q_blk, k_blk, v_blk, t):
    b, s, _ = proj3.shape
    n_blocks = s // MOBA_BLOCK
    assert n_blocks <= HEAD_DIM, "one lane per key block"
    qi, kj = _pair_tables(s // t)
    kern = functools.partial(_moba_kernel, t=t, n_pairs=qi.shape[0], n_blocks=n_blocks)
    smem = pl.BlockSpec(memory_space=pltpu.SMEM)
    return pl.pallas_call(
        kern,
        out_shape=jax.ShapeDtypeStruct((b, s, n_heads * HEAD_DIM), BF16),
        grid=(b, n_heads),
        in_specs=[smem, smem, smem,
                  _head_spec(s, q_blk), _head_spec(s, k_blk), _head_spec(s, v_blk)],
        out_specs=_head_spec(s, 0),
        scratch_shapes=[pltpu.VMEM((HEAD_DIM, HEAD_DIM), F32),
                        pltpu.VMEM((s, 2 * HEAD_DIM), BF16),
                        pltpu.VMEM((s, 2 * HEAD_DIM), BF16)] + _attn_scratch(s, t, 1),
        compiler_params=_params("parallel", "parallel"),
        name="moba_attention",
    )(slopes, qi, kj, proj3, proj3, proj3)


def _pack_cols(a, b):
    ua = lax.bitcast_convert_type(a, jnp.uint32)
    ub = lax.bitcast_convert_type(b, jnp.uint32)
    return (ua >> 16) | (ub & jnp.uint32(0xFFFF0000))


def _unpack_cols(w):
    lo = lax.bitcast_convert_type(w << 16, F32)
    hi = lax.bitcast_convert_type(w & jnp.uint32(0xFFFF0000), F32)
    return lo.astype(BF16), hi.astype(BF16)


def _lanes_from_columns(cols, dtype):
    tm = cols[0].shape[0]
    lane = lax.broadcasted_iota(jnp.int32, (tm, len(cols)), 1)
    out = jnp.zeros((tm, len(cols)), dtype)
    for r, c in enumerate(cols):
        out = jnp.where(lane == r, c.astype(dtype), out)
    return out


def _router_kernel(h_ref, w_ref, wr_ref, br_ref, xp_ref, ids_ref, gates_ref):
    x = h_ref[...]
    ms = jnp.mean(x * x, axis=-1, keepdims=True)
    n2 = (x * lax.rsqrt(ms + NORM_EPS)) * w_ref[...]
    hi = n2.astype(BF16)
    hi_f = hi.astype(F32)
    half = n2.shape[1] // 2
    xp_ref[...] = _pack_cols(hi_f[:, :half], hi_f[:, half:])

    lo = (n2 - hi_f).astype(BF16)
    wr = wr_ref[...]
    wr_hi = wr.astype(BF16)
    wr_lo = (wr - wr_hi.astype(F32)).astype(BF16)
    logits = (jnp.dot(hi, wr_hi, preferred_element_type=F32)
              + jnp.dot(lo, wr_hi, preferred_element_type=F32)
              + jnp.dot(hi, wr_lo, preferred_element_type=F32)) + br_ref[...]

    n_exp = logits.shape[1]
    eid = lax.broadcasted_iota(jnp.int32, logits.shape, 1).astype(F32)
    neg = jnp.full_like(logits, -jnp.inf)
    g = logits
    vals, idxs = [], []
    for _ in range(TOP_K):
        mx = jnp.max(g, axis=-1, keepdims=True)
        first = jnp.min(jnp.where(g == mx, eid, float(n_exp)), axis=-1, keepdims=True)
        vals.append(mx)
        idxs.append(first)
        g = jnp.where(eid == first, neg, g)
    exps = [jnp.exp(v - vals[0]) for v in vals]
    denom = exps[0] + exps[1] + exps[2] + exps[3]
    ids_ref[...] = _lanes_from_columns(idxs, jnp.int32)
    gates_ref[...] = _lanes_from_columns([e / denom for e in exps], F32)


def _router(h, norm_w, w_router, b_router, tm):
    t, d = h.shape
    n_exp = w_router.shape[1]
    return pl.pallas_call(
        _router_kernel,
        out_shape=(jax.ShapeDtypeStruct((t, d // 2), jnp.uint32),
                   jax.ShapeDtypeStruct((t, TOP_K), jnp.int32),
                   jax.ShapeDtypeStruct((t, TOP_K), F32)),
        grid=(t // tm,),
        in_specs=[pl.BlockSpec((tm, d), lambda i: (i, 0)),
                  pl.BlockSpec((1, d), lambda i: (0, 0)),
                  pl.BlockSpec((d, n_exp), lambda i: (0, 0)),
                  pl.BlockSpec((1, n_exp), lambda i: (0, 0))],
        out_specs=(pl.BlockSpec((tm, d // 2), lambda i: (i, 0)),
                   pl.BlockSpec((tm, TOP_K), lambda i: (i, 0)),
                   pl.BlockSpec((tm, TOP_K), lambda i: (i, 0))),
        compiler_params=_params("parallel"),
        name="rmsnorm2_router",
    )(h, norm_w.reshape(1, d), w_router, b_router.reshape(1, n_exp))


def _rank_kernel(ids_ref, rank_ref, counts_ref, carry_ref, *, n_exp):
    i = pl.program_id(0)

    @pl.when(i == 0)
    def _():
        carry_ref[...] = jnp.zeros_like(carry_ref)

    ids = ids_ref[...]
    tm = ids.shape[0]
    eid = lax.broadcasted_iota(jnp.int32, (tm, n_exp), 1)
    onehots = [(eid == ids[:, r:r + 1]).astype(F32) for r in range(TOP_K)]
    total = onehots[0] + onehots[1] + onehots[2] + onehots[3]
    row = lax.broadcasted_iota(jnp.int32, (tm, tm), 0)
    col = lax.broadcasted_iota(jnp.int32, (tm, tm), 1)
    strict_lower = (col < row).astype(BF16)
    before = jnp.dot(strict_lower, total.astype(BF16), preferred_element_type=F32)
    before = before + carry_ref[...]
    ranks = [jnp.sum(oh * before, axis=-1, keepdims=True) for oh in onehots]
    rank_ref[...] = _lanes_from_columns(ranks, jnp.int32)
    carry_ref[...] += jnp.sum(total, axis=0, keepdims=True)
    counts_ref[...] = carry_ref[...]


def _rank(ids, n_exp, tm):
    t = ids.shape[0]
    return pl.pallas_call(
        functools.partial(_rank_kernel, n_exp=n_exp),
        out_shape=(jax.ShapeDtypeStruct((t, TOP_K), jnp.int32),
                   jax.ShapeDtypeStruct((1, n_exp), F32)),
        grid=(t // tm,),
        in_specs=[pl.BlockSpec((tm, TOP_K), lambda i: (i, 0))],
        out_specs=(pl.BlockSpec((tm, TOP_K), lambda i: (i, 0)),
                   pl.BlockSpec((1, n_exp), lambda i: (0, 0))),
        scratch_shapes=[pltpu.VMEM((1, n_exp), F32)],
        compiler_params=_params("arbitrary"),
        name="expert_rank",
    )(ids)


def _dispatch_kernel(fill_start_ref, fill_n_ref, pos_ref, x_ref, buf_ref,
                     zeros_ref, sem, zsem, *, n_exp, tile_rows):
    i = pl.program_id(0)
    tm = x_ref.shape[0]
    zrows = zeros_ref.shape[0]
    sub = SUBLANES
    n_bits = (zrows // sub - 1).bit_length()

    def head_copy(e, r):
        a = fill_start_ref[e]
        n_head = jnp.minimum((-a) & (sub - 1), fill_n_ref[e])
        return r < n_head, pltpu.make_async_copy(
            zeros_ref.at[pl.ds(0, 1), :], buf_ref.at[pl.ds(a + r, 1), :], zsem)

    def body_copy(e, bit):
        a = fill_start_ref[e]
        n_head = jnp.minimum((-a) & (sub - 1), fill_n_ref[e])
        groups = (fill_n_ref[e] - n_head) // sub
        size = sub << bit
        dst = pl.multiple_of(a + n_head + sub * (groups & ((1 << bit) - 1)), sub)
        return (groups & (1 << bit)) != 0, pltpu.make_async_copy(
            zeros_ref.at[pl.ds(0, size), :], buf_ref.at[pl.ds(dst, size), :], zsem)

    def fill_copies():
        for e in range(n_exp):
            for r in range(sub - 1):
                yield head_copy(e, r)
            for bit in range(n_bits):
                yield body_copy(e, bit)

    def tail_copy(c):
        dst = pl.multiple_of(c * zrows, zrows)
        return pltpu.make_async_copy(zeros_ref, buf_ref.at[pl.ds(dst, zrows), :], zsem)

    @pl.when(i == 0)
    def _():
        zeros_ref[...] = jnp.zeros(zeros_ref.shape, zeros_ref.dtype)
        first_tail = fill_start_ref[n_exp] // zrows
        n_chunks = buf_ref.shape[0] // zrows

        def tail_start(c, carry):
            tail_copy(c).start()
            return carry

        def tail_wait(c, carry):
            tail_copy(c).wait()
            return carry

        for live, cp in fill_copies():
            pl.when(live)(cp.start)
        lax.fori_loop(first_tail, n_chunks, tail_start, 0)
        for live, cp in fill_copies():
            pl.when(live)(cp.wait)
        lax.fori_loop(first_tail, n_chunks, tail_wait, 0)

    def start(tok, carry):
        src = x_ref.at[pl.ds(tok, 1), :]
        for slot in range(TOP_K):
            dst = buf_ref.at[pl.ds(pos_ref[tok * TOP_K + slot], 1), :]
            pltpu.make_async_copy(src, dst, sem).start()
        return carry

    lax.fori_loop(0, tm, start, 0, unroll=DMA_ISSUE_UNROLL)
    for _ in range(TOP_K):
        pltpu.make_async_copy(x_ref, buf_ref.at[pl.ds(0, tm), :], sem).wait()


def _dispatch(fill_start, fill_n, pos_flat, xp, n_rows, tm, tile_rows):
    t, w = xp.shape
    n_exp = fill_n.shape[0]
    zrows = tile_rows
    grid_spec = pltpu.PrefetchScalarGridSpec(
        num_scalar_prefetch=2,
        grid=(t // tm,),
        in_specs=[pl.BlockSpec((tm * TOP_K,), lambda i, fs, fn: (i,), memory_space=pltpu.SMEM),
                  pl.BlockSpec((tm, w), lambda i, fs, fn: (i, 0))],
        out_specs=pl.BlockSpec(memory_space=pl.ANY),
        scratch_shapes=[pltpu.VMEM((zrows, w), xp.dtype),
                        pltpu.SemaphoreType.DMA(()),
                        pltpu.SemaphoreType.DMA(())],
    )
    return pl.pallas_call(
        functools.partial(_dispatch_kernel, n_exp=n_exp, tile_rows=tile_rows),
        out_shape=jax.ShapeDtypeStruct((n_rows, w), xp.dtype),
        grid_spec=grid_spec,
        compiler_params=_params("arbitrary"),
        name="dispatch",
    )(fill_start, fill_n, pos_flat, xp)


def _deinterleave_matrix(n):
    r = lax.broadcasted_iota(jnp.int32, (n, n), 0)
    c = lax.broadcasted_iota(jnp.int32, (n, n), 1)
    src = jnp.where(c < n // 2, 2 * c, 2 * (c - n // 2) + 1)
    return (r == src).astype(BF16)


def _stage_expert_weights(w_hbm, stage_ref, sem, tabs, convert):
    te_ref, tf_ref, ne_ref, nw_ref = tabs
    c = pl.program_id(0)
    i = pl.program_id(1)
    width = stage_ref.shape[1]

    def block_copy(e, chunk):
        col = pl.multiple_of(chunk * width, width)
        return pltpu.make_async_copy(w_hbm.at[e, :, pl.ds(col, width)], stage_ref, sem)

    @pl.when((c == 0) & (i == 0))
    def _():
        block_copy(te_ref[0], 0).start()

    @pl.when(tf_ref[i] == 1)
    def _():
        block_copy(te_ref[i], c).wait()
        convert()
        next_chunk = c + nw_ref[i]

        @pl.when(next_chunk < pl.num_programs(0))
        def _():
            block_copy(ne_ref[i], next_chunk).start()


def _gateup_kernel(te_ref, tv_ref, tf_ref, tr_ref, ne_ref, nw_ref, x_ref, w_hbm, bg_ref, bu_ref,
                   o_ref, stage_ref, wg_ref, wu_ref, sem):
    del tr_ref
    i = pl.program_id(1)
    grp = V7X_MXU_DIM
    half = grp // 2

    def convert():
        perm = _deinterleave_matrix(grp)
        for g in range(stage_ref.shape[1] // grp):
            w = stage_ref[:, g * grp:(g + 1) * grp].astype(BF16)
            wp = jnp.dot(w, perm, preferred_element_type=F32).astype(BF16)
            wg_ref[:, g * half:(g + 1) * half] = wp[:, :half]
            wu_ref[:, g * half:(g + 1) * half] = wp[:, half:]

    _stage_expert_weights(w_hbm, stage_ref, sem, (te_ref, tf_ref, ne_ref, nw_ref), convert)

    @pl.when(tv_ref[i] == 1)
    def _():
        xa, xb = _unpack_cols(x_ref[...])
        k_half = xa.shape[1]

        def proj(w_s, b_ref):
            return (jnp.dot(xa, w_s[:k_half, :], preferred_element_type=F32)
                    + jnp.dot(xb, w_s[k_half:, :], preferred_element_type=F32)
                    + b_ref[...])

        g = jnp.minimum(proj(wg_ref, bg_ref), SWIGLU_LIMIT)
        u = jnp.clip(proj(wu_ref, bu_ref), -SWIGLU_LIMIT, SWIGLU_LIMIT)
        sig = 1.0 / (1.0 + jnp.exp(-(g * SWIGLU_ALPHA)))
        o_ref[...] = ((u + 1.0) * (g * sig)).astype(o_ref.dtype)

    @pl.when(tv_ref[i] == 0)
    def _():
        o_ref[...] = jnp.zeros_like(o_ref)


def _gateup(tabs, xs, w_gu, bg, bu, tm, fc):
    rows, k_half = xs.shape
    n_exp, d, f2 = w_gu.shape
    f = f2 // 2
    n_tiles = rows // tm
    n_chunks = f // fc

    def b_map(c, i, te, tv, tf, tr, ne, nw):
        return (te[i], 0, c)

    grid_spec = pltpu.PrefetchScalarGridSpec(
        num_scalar_prefetch=6,
        grid=(n_chunks, n_tiles),
        in_specs=[pl.BlockSpec((tm, k_half), lambda c, i, te, tv, tf, tr, ne, nw: (tr[i], 0)),
                  pl.BlockSpec(memory_space=pl.ANY),
                  pl.BlockSpec((None, 1, fc), b_map),
                  pl.BlockSpec((None, 1, fc), b_map)],
        out_specs=pl.BlockSpec((tm, fc), lambda c, i, te, tv, tf, tr, ne, nw: (i, c)),
        scratch_shapes=[pltpu.VMEM((d, 2 * fc), F32),
                        pltpu.VMEM((d, fc), BF16), pltpu.VMEM((d, fc), BF16),
                        pltpu.SemaphoreType.DMA(())],
    )
    return pl.pallas_call(
        _gateup_kernel,
        out_shape=jax.ShapeDtypeStruct((rows, f), BF16),
        grid_spec=grid_spec,
        compiler_params=_params("arbitrary", "arbitrary"),
        name="expert_gate_up",
    )(*tabs, xs, w_gu, bg, bu)


def _down_kernel(te_ref, tv_ref, tf_ref, ne_ref, nw_ref, a_ref, w_hbm, b_ref, o_ref,
                 stage_ref, wb_ref, sem):
    i = pl.program_id(1)

    def convert():
        wb_ref[...] = stage_ref[...].astype(BF16)

    _stage_expert_weights(w_hbm, stage_ref, sem, (te_ref, tf_ref, ne_ref, nw_ref), convert)

    @pl.when(tv_ref[i] == 1)
    def _():
        o_ref[...] = jnp.dot(a_ref[...], wb_ref[...], preferred_element_type=F32) + b_ref[...]

    @pl.when(tv_ref[i] == 0)
    def _():
        o_ref[...] = jnp.zeros_like(o_ref)


def _down(tabs, act, wd, bd, tm, tn):
    rows, f = act.shape
    n_exp, _, d = wd.shape
    n_chunks = d // tn

    grid_spec = pltpu.PrefetchScalarGridSpec(
        num_scalar_prefetch=5,
        grid=(n_chunks, rows // tm),
        in_specs=[pl.BlockSpec((tm, f), lambda c, i, te, tv, tf, ne, nw: (i, 0)),
                  pl.BlockSpec(memory_space=pl.ANY),
                  pl.BlockSpec((None, 1, tn), lambda c, i, te, tv, tf, ne, nw: (te[i], 0, c))],
        out_specs=pl.BlockSpec((tm, tn), lambda c, i, te, tv, tf, ne, nw: (i, c)),
        scratch_shapes=[pltpu.VMEM((f, tn), F32), pltpu.VMEM((f, tn), BF16),
                        pltpu.SemaphoreType.DMA(())],
    )
    return pl.pallas_call(
        _down_kernel,
        out_shape=jax.ShapeDtypeStruct((rows, d), F32),
        grid_spec=grid_spec,
        compiler_params=_params("arbitrary", "arbitrary"),
        name="expert_down",
    )(*tabs, act, wd, bd)


def _combine_kernel(pos_ref, gates_ref, h_ref, w_ref, y_ref, o_ref, rows_ref, sem):
    tm = h_ref.shape[0]

    def start(tok, carry):
        for slot in range(TOP_K):
            src = y_ref.at[pl.ds(pos_ref[tok * TOP_K + slot], 1), :]
            pltpu.make_async_copy(src, rows_ref.at[slot, pl.ds(tok, 1), :], sem).start()
        return carry

    lax.fori_loop(0, tm, start, 0, unroll=DMA_ISSUE_UNROLL)
    for r in range(TOP_K):
        pltpu.make_async_copy(y_ref.at[pl.ds(0, tm), :], rows_ref.at[r], sem).wait()

    gates = gates_ref[...]
    moe = gates[:, 0:1] * rows_ref[0]
    for r in range(1, TOP_K):
        moe += gates[:, r:r + 1] * rows_ref[r]
    x = h_ref[...] + moe
    ms = jnp.mean(x * x, axis=-1, keepdims=True)
    o_ref[...] = (x * lax.rsqrt(ms + NORM_EPS)) * w_ref[...]


def _combine(pos_flat, gates, h, norm_w, ys, tm):
    t, d = h.shape
    return pl.pallas_call(
        _combine_kernel,
        out_shape=jax.ShapeDtypeStruct((t, d), F32),
        grid=(t // tm,),
        in_specs=[pl.BlockSpec((tm * TOP_K,), lambda i: (i,), memory_space=pltpu.SMEM),
                  pl.BlockSpec((tm, TOP_K), lambda i: (i, 0)),
                  pl.BlockSpec((tm, d), lambda i: (i, 0)),
                  pl.BlockSpec((1, d), lambda i: (0, 0)),
                  pl.BlockSpec(memory_space=pl.ANY)],
        out_specs=pl.BlockSpec((tm, d), lambda i: (i, 0)),
        scratch_shapes=[pltpu.VMEM((TOP_K, tm, d), F32),
                        pltpu.SemaphoreType.DMA(())],
        compiler_params=_params("arbitrary"),
        name="combine_final_norm",
    )(pos_flat, gates, h, norm_w.reshape(1, d), ys)


def _tiles(t, s, d, f):
    return dict(
        norm_rows=min(256, t),
        mm_rows=min(1024, t),
        mm_cols=min(1024, d),
        attn_tile=min(512, s),
        router_rows=min(256, t),
        rank_rows=min(512, t),
        dispatch_rows=min(256, t),
        expert_rows=min(512, t),
        ff_chunk=min(512, f),
        down_cols=min(2048, d),
        combine_rows=min(256, t),
    )


def _alibi_slopes(n):
    return jnp.exp2(-ALIBI_MAX_BIAS * jnp.arange(1, n + 1, dtype=F32) / n)


def _routing_tables(ids, rank, counts, tm, n_tiles):
    n_exp = counts.shape[0]
    i32 = jnp.int32
    padded = ((counts + tm - 1) // tm) * tm
    ends = jnp.cumsum(padded)
    starts = ends - padded
    onehot = ids[..., None] == jnp.arange(n_exp, dtype=i32)
    pos = rank + jnp.sum(jnp.where(onehot, starts, 0), axis=-1)
    tile_start = jnp.arange(n_tiles, dtype=i32) * tm
    tile_valid = (tile_start < ends[-1]).astype(i32)
    tile_expert = jnp.minimum(jnp.sum((tile_start[:, None] >= ends[None, :]).astype(i32), axis=1),
                              n_exp - 1)
    prev_expert = jnp.concatenate([jnp.full((1,), -1, i32), tile_expert[:-1]])
    tile_first = tile_valid * (tile_expert != prev_expert).astype(i32)
    tile_idx = jnp.arange(n_tiles, dtype=i32)
    later_first = (tile_idx[None, :] > tile_idx[:, None]) & (tile_first[None, :] == 1)
    next_first = jnp.min(jnp.where(later_first, tile_idx[None, :], n_tiles), axis=1)
    next_wraps = (next_first == n_tiles).astype(i32)
    next_expert = jnp.sum(jnp.where(tile_idx[None, :] == (next_first % n_tiles)[:, None],
                                    tile_expert[None, :], 0), axis=1).astype(i32)
    tile_row = jnp.minimum(jnp.arange(n_tiles, dtype=i32), ends[-1] // tm - 1)
    fill_start = jnp.concatenate([starts + counts, ends[-1:]]).astype(i32)
    fill_n = (padded - counts).astype(i32)
    tiles = dict(expert=tile_expert, valid=tile_valid, first=tile_first, row=tile_row,
                 next_expert=next_expert, next_wraps=next_wraps)
    return pos.reshape(-1), tiles, fill_start, fill_n


def _layer(h, l, norm1_w, w_in, lam_q1, lam_k1, lam_q2, lam_k2, subln_w, w_out, norm2_w,
           w_router, b_router, w_gate_up, b_gate_up, w_down, b_down, out_norm_w):
    b, s, d = h.shape
    t = b * s
    n_exp = w_router.shape[-1]
    f = w_down.shape[-2]
    n_heads = d // 2 // HEAD_DIM
    width = n_heads * HEAD_DIM
    tl = _tiles(t, s, d, f)
    lambda_init = 0.8 - 0.6 * math.exp(-0.3 * l)
    x2 = h.reshape(t, d)

    n1 = _rmsnorm(x2, norm1_w[l], tl["norm_rows"])
    n_cols = w_in.shape[-1]
    col_scale = jnp.ones((n_cols,), F32)
    col_scale = col_scale.at[:width].set(LOG2_E * DIFF_QK_DIM ** -0.5)
    col_scale = col_scale.at[3 * width:4 * width].set(LOG2_E * HEAD_DIM ** -0.5)
    proj = _matmul(n1, w_in[l].astype(BF16), col_scale.reshape(1, n_cols), tl["mm_rows"],
                   tl["mm_cols"], BF16)
    proj3 = proj.reshape(b, s, proj.shape[1])
    slopes = _alibi_slopes(n_heads)
    lam4 = jnp.stack([lam_q1[l], lam_k1[l], lam_q2[l], lam_k2[l]]).astype(F32)
    blocks = width // HEAD_DIM
    y_a = _diff_attention(proj3, slopes, lam4, subln_w[l].reshape(1, HEAD_DIM).astype(F32),
                          n_heads, 0, blocks, 2 * blocks, tl["attn_tile"], lambda_init)
    y_b = _moba_attention(proj3, slopes, n_heads, 3 * blocks, 4 * blocks, 5 * blocks,
                          tl["attn_tile"])
    h1 = _outproj(y_a.reshape(t, width), y_b.reshape(t, width), w_out[l].astype(BF16), x2,
                  tl["mm_rows"], tl["mm_cols"])

    xp, ids, gates = _router(h1, norm2_w[l], w_router[l], b_router[l], tl["router_rows"])
    rank, counts = _rank(ids, n_exp, tl["rank_rows"])
    tm = tl["expert_rows"]
    n_tiles = (t * TOP_K) // tm + n_exp
    pos_flat, tiles, fill_start, fill_n = _routing_tables(
        ids, rank, counts.reshape(n_exp).astype(jnp.int32), tm, n_tiles)

    xs = _dispatch(fill_start, fill_n, pos_flat, xp, n_tiles * tm, tl["dispatch_rows"], tm)
    bgu = b_gate_up[l].reshape(n_exp, 1, f, 2)
    gateup_tabs = (tiles["expert"], tiles["valid"], tiles["first"], tiles["row"],
                   tiles["next_expert"], tiles["next_wraps"])
    act = _gateup(gateup_tabs, xs, w_gate_up[l], bgu[..., 0], bgu[..., 1], tm, tl["ff_chunk"])
    down_tabs = (tiles["expert"], tiles["valid"], tiles["first"],
                 tiles["next_expert"], tiles["next_wraps"])
    ys = _down(down_tabs, act, w_down[l], b_down[l].reshape(n_exp, 1, d), tm, tl["down_cols"])
    return _combine(pos_flat, gates, h1, out_norm_w, ys, tl["combine_rows"]).reshape(b, s, d)


def kernel(x, norm1_w, w_in, lam_q1, lam_k1, lam_q2, lam_k2, subln_w, w_out, norm2_w,
           w_router, b_router, w_gate_up, b_gate_up, w_down, b_down, final_norm_w):
    depth = w_in.shape[0]
    assert depth == 1, "the fused combine + final norm stage assumes a single layer"
    return _layer(x, 0, norm1_w, w_in, lam_q1, lam_k1, lam_q2, lam_k2, subln_w, w_out,
                  norm2_w, w_router, b_router, w_gate_up, b_gate_up, w_down, b_down,
                  final_norm_w)
```

```python
import functools
import math

import jax
import jax.numpy as jnp
from jax import lax
from jax.experimental import pallas as pl
from jax.experimental.pallas import tpu as pltpu

F32 = jnp.float32
BF16 = jnp.bfloat16

HEAD_DIM = 128
DIFF_QK_DIM = HEAD_DIM // 2
MOBA_BLOCK = 256
MOBA_TOPK = 3
TOP_K = 4
SWIGLU_LIMIT = 7.0
SWIGLU_ALPHA = 1.702
NORM_EPS = 1e-5
ALIBI_MAX_BIAS = 8.0

V7X_VMEM_BYTES = 64 * 1024 * 1024
VMEM_LIMIT_BYTES = V7X_VMEM_BYTES - 8 * 1024 * 1024
V7X_MXU_DIM = 256
SUBLANES = 8
DMA_ISSUE_UNROLL = 8


def _params(*semantics):
    return pltpu.CompilerParams(dimension_semantics=semantics,
                                vmem_limit_bytes=VMEM_LIMIT_BYTES)


def _rmsnorm_kernel(x_ref, w_ref, o_ref):
    x = x_ref[...]
    ms = jnp.mean(x * x, axis=-1, keepdims=True)
    o_ref[...] = ((x * lax.rsqrt(ms + NORM_EPS)) * w_ref[...]).astype(o_ref.dtype)


def _rmsnorm(x, w, tm):
    t, d = x.shape
    return pl.pallas_call(
        _rmsnorm_kernel,
        out_shape=jax.ShapeDtypeStruct((t, d), BF16),
        grid=(t // tm,),
        in_specs=[pl.BlockSpec((tm, d), lambda i: (i, 0)),
                  pl.BlockSpec((1, d), lambda i: (0, 0))],
        out_specs=pl.BlockSpec((tm, d), lambda i: (i, 0)),
        compiler_params=_params("parallel"),
        name="rmsnorm1",
    )(x, w.reshape(1, d))


def _matmul_kernel(a_ref, b_ref, cs_ref, o_ref):
    acc = jnp.dot(a_ref[...], b_ref[...], preferred_element_type=F32)
    o_ref[...] = (acc * cs_ref[...]).astype(o_ref.dtype)


def _matmul(a, b, col_scale, tm, tn, out_dtype):
    m, k = a.shape
    _, n = b.shape
    return pl.pallas_call(
        _matmul_kernel,
        out_shape=jax.ShapeDtypeStruct((m, n), out_dtype),
        grid=(m // tm, n // tn),
        in_specs=[pl.BlockSpec((tm, k), lambda i, j: (i, 0)),
                  pl.BlockSpec((k, tn), lambda i, j: (0, j)),
                  pl.BlockSpec((1, tn), lambda i, j: (0, j))],
        out_specs=pl.BlockSpec((tm, tn), lambda i, j: (i, j)),
        compiler_params=_params("parallel", "parallel"),
        name="in_proj",
    )(a, b, col_scale)


def _outproj_kernel(ya_ref, yb_ref, w_ref, x_ref, o_ref):
    ka = ya_ref.shape[1]
    acc = jnp.dot(ya_ref[...], w_ref[:ka, :], preferred_element_type=F32)
    acc += jnp.dot(yb_ref[...], w_ref[ka:, :], preferred_element_type=F32)
    o_ref[...] = x_ref[...] + acc


def _outproj(ya, yb, w, x, tm, tn):
    m, ka = ya.shape
    kb = yb.shape[1]
    n = w.shape[1]
    return pl.pallas_call(
        _outproj_kernel,
        out_shape=jax.ShapeDtypeStruct((m, n), F32),
        grid=(m // tm, n // tn),
        in_specs=[pl.BlockSpec((tm, ka), lambda i, j: (i, 0)),
                  pl.BlockSpec((tm, kb), lambda i, j: (i, 0)),
                  pl.BlockSpec((ka + kb, tn), lambda i, j: (0, j)),
                  pl.BlockSpec((tm, tn), lambda i, j: (i, j))],
        out_specs=pl.BlockSpec((tm, tn), lambda i, j: (i, j)),
        compiler_params=_params("parallel", "parallel"),
        name="out_proj",
    )(ya, yb, w, x)


RUNNING_MAX_INIT = -1e30
LOG2_E = 1.4426950408889634


def _qk(q, k):
    return lax.dot_general(q, k, (((1,), (1,)), ((), ())), preferred_element_type=F32)


def _pair_tables(n_tiles):
    qi = [i for i in range(n_tiles) for _ in range(i + 1)]
    kj = [j for i in range(n_tiles) for j in range(i + 1)]
    return jnp.asarray(qi, jnp.int32), jnp.asarray(kj, jnp.int32)


def _fill_vext(vext_ref, v_ref):
    vext_ref[:, :HEAD_DIM] = v_ref[...]
    vext_ref[:, HEAD_DIM:] = jnp.ones((v_ref.shape[0], HEAD_DIM), vext_ref.dtype)


def _fill_tile_bias(bias_ref, col_bias_scale, t):
    row = lax.broadcasted_iota(jnp.int32, (t, t), 0)
    col = lax.broadcasted_iota(jnp.int32, (t, t), 1)
    cb = col_bias_scale * col.astype(F32)
    bias_ref[0] = cb
    bias_ref[1] = jnp.where(col <= row, cb, -jnp.inf)


def _attn_scratch(s_len, t, n_comp):
    return [pltpu.VMEM((s_len, 2 * HEAD_DIM), BF16),
            pltpu.VMEM((2, t, t), F32),
            pltpu.VMEM((n_comp, t, t), F32),
            pltpu.VMEM((n_comp, t, t), F32),
            pltpu.VMEM((n_comp, t, t), BF16),
            pltpu.VMEM((n_comp, t, t), BF16),
            pltpu.VMEM((n_comp, t, 1), F32),
            pltpu.VMEM((n_comp, t, 1), F32),
            pltpu.VMEM((n_comp, t, 1), F32),
            pltpu.VMEM((n_comp, t, 2 * HEAD_DIM), F32),
            pltpu.VMEM((s_len // t + 1, n_comp, t, 2 * HEAD_DIM), F32)]


def _attn_pipeline(qi_ref, kj_ref, n_pairs, t, n_comp, scores, shift_step, rows_can_be_empty,
                   vext_ref, bufs, finalize):
    s_bufs, p_bufs, a_bufs, m_ref, acc_ref, done_ref = bufs
    for buf in s_bufs + p_bufs:
        buf[...] = jnp.zeros(buf.shape, buf.dtype)
    for buf in a_bufs:
        buf[...] = jnp.ones(buf.shape, buf.dtype)
    m_ref[...] = jnp.full(m_ref.shape, RUNNING_MAX_INIT, F32)
    acc_ref[...] = jnp.zeros(acc_ref.shape, F32)

    def pair(n):
        nc = jnp.clip(n, 0, n_pairs - 1)
        return qi_ref[nc], kj_ref[nc], (n >= 0) & (n < n_pairs)

    def stage_scores(n, s_buf):
        i, j, _ = pair(n)
        for c in range(n_comp):
            s_buf[c] = scores(i, j, c)

    def stage_softmax(n, s_buf, p_buf, a_buf):
        _, j, _ = pair(n)
        first = j == 0
        max_cap = jnp.where(first, RUNNING_MAX_INIT, jnp.inf)
        keep = jnp.where(first, 0.0, 1.0)
        for c in range(n_comp):
            m_prev = jnp.minimum(m_ref[c], max_cap)
            m_cur = jnp.maximum(m_prev, jnp.max(s_buf[c], axis=-1, keepdims=True))
            m_ref[c] = m_cur - shift_step
            p_buf[c] = jnp.exp2(s_buf[c] - m_cur).astype(p_buf.dtype)
            rescale = jnp.exp2(m_prev - m_cur)
            a_buf[c] = rescale * keep if rows_can_be_empty else rescale

    n_tiles = done_ref.shape[0] - 1

    def stage_values(n, p_buf, a_buf):
        i, j, active = pair(n)
        vext = vext_ref[pl.ds(pl.multiple_of(j * t, t), t), :]
        slot = jnp.where(active, i, n_tiles)
        for c in range(n_comp):
            acc = a_buf[c] * acc_ref[c] + jnp.dot(p_buf[c], vext, preferred_element_type=F32)
            acc_ref[c] = acc
            done_ref[slot, c] = acc

    def body(step, carry):
        for parity in range(2):
            n = 2 * step + parity
            stage_values(n - 2, p_bufs[parity], a_bufs[parity])
            stage_scores(n, s_bufs[parity])
            stage_softmax(n - 1, s_bufs[1 - parity], p_bufs[1 - parity], a_bufs[1 - parity])
        return carry

    lax.fori_loop(0, (n_pairs + 3) // 2, body, 0)
    for i in range(n_tiles):
        finalize(i, done_ref[i])


def _split_bufs(scratch):
    s0, s1, p0, p1, a0, a1, m_ref, acc_ref, done_ref = scratch
    return (s0, s1), (p0, p1), (a0, a1), m_ref, acc_ref, done_ref


def _diff_attn_kernel(slopes_ref, qi_ref, kj_ref, lam_ref, subw_ref, q_ref, k_ref, v_ref, o_ref,
                      qc_ref, vext_ref, bias_ref, *scratch, t, n_pairs, lambda_init):
    slope = slopes_ref[pl.program_id(1)]
    dq = DIFF_QK_DIM
    _fill_vext(vext_ref, v_ref)
    _fill_tile_bias(bias_ref, LOG2_E * slope, t)

    q_all = q_ref[...]
    lane = lax.broadcasted_iota(jnp.int32, q_all.shape, 1)
    qc_ref[0] = jnp.where(lane < dq, q_all, jnp.zeros_like(q_all))
    qc_ref[1] = jnp.where(lane >= dq, q_all, jnp.zeros_like(q_all))

    def scores(i, j, c):
        q = qc_ref[c, pl.ds(pl.multiple_of(i * t, t), t), :]
        k = k_ref[pl.ds(pl.multiple_of(j * t, t), t), :]
        return _qk(q, k) + bias_ref[(i == j).astype(jnp.int32)]

    bufs = _split_bufs(scratch)

    def finalize(i, acc):
        lam_v = lam_ref[...]
        lam = (jnp.exp(jnp.sum(lam_v[0:1, :] * lam_v[1:2, :], axis=-1, keepdims=True))
               - jnp.exp(jnp.sum(lam_v[2:3, :] * lam_v[3:4, :], axis=-1, keepdims=True))
               + lambda_init)
        a1 = acc[0]
        a2 = acc[1]
        o = (a1[:, :HEAD_DIM] / a1[:, HEAD_DIM:HEAD_DIM + 1]
             - lam * (a2[:, :HEAD_DIM] / a2[:, HEAD_DIM:HEAD_DIM + 1]))
        ms = jnp.mean(o * o, axis=-1, keepdims=True)
        o = (o * lax.rsqrt(ms + NORM_EPS)) * subw_ref[...]
        o_ref[i * t:(i + 1) * t, :] = (o * (1.0 - lambda_init)).astype(o_ref.dtype)

    _attn_pipeline(qi_ref, kj_ref, n_pairs, t, 2, scores, LOG2_E * slope * t, False,
                   vext_ref, bufs, finalize)


def _head_spec(s, blk):
    return pl.BlockSpec((None, s, HEAD_DIM), lambda bi, h: (bi, 0, blk + h))


def _diff_attention(proj3, slopes, lam4, subw, n_heads, q_blk, k_blk, v_blk, t, lambda_init):
    b, s, _ = proj3.shape
    qi, kj = _pair_tables(s // t)
    kern = functools.partial(_diff_attn_kernel, t=t, n_pairs=qi.shape[0], lambda_init=lambda_init)
    smem = pl.BlockSpec(memory_space=pltpu.SMEM)
    return pl.pallas_call(
        kern,
        out_shape=jax.ShapeDtypeStruct((b, s, n_heads * HEAD_DIM), BF16),
        grid=(b, n_heads),
        in_specs=[smem, smem, smem,
                  pl.BlockSpec((4, DIFF_QK_DIM), lambda bi, h: (0, 0)),
                  pl.BlockSpec((1, HEAD_DIM), lambda bi, h: (0, 0)),
                  _head_spec(s, q_blk), _head_spec(s, k_blk), _head_spec(s, v_blk)],
        out_specs=_head_spec(s, 0),
        scratch_shapes=[pltpu.VMEM((2, s, HEAD_DIM), BF16)] + _attn_scratch(s, t, 2),
        compiler_params=_params("parallel", "parallel"),
        name="diff_attention",
    )(slopes, qi, kj, lam4, subw, proj3, proj3, proj3)


MASKED_SCORE = -(2.0 ** 127)


def _moba_kernel(slopes_ref, qi_ref, kj_ref, q_ref, k_ref, v_ref, o_ref,
                 kmean_ref, qaug_ref, kaug_ref, vext_ref, bias_ref, *scratch,
                 t, n_pairs, n_blocks):
    slope = slopes_ref[pl.program_id(1)]
    blk = MOBA_BLOCK
    blk_shift = blk.bit_length() - 1
    s_len = q_ref.shape[0]

    _fill_vext(vext_ref, v_ref)
    _fill_tile_bias(bias_ref, LOG2_E * slope, t)

    kmean_ref[...] = jnp.zeros(kmean_ref.shape, F32)
    for n in range(n_blocks):
        kb = k_ref[n * blk:(n + 1) * blk, :].astype(F32)
        kmean_ref[n:n + 1, :] = jnp.mean(kb, axis=0, keepdims=True)
    km = kmean_ref[...]
    km_hi = km.astype(BF16)
    km_lo = (km - km_hi.astype(F32)).astype(BF16)

    nb_pad = -(-n_blocks // SUBLANES) * SUBLANES
    blk_row = lax.broadcasted_iota(jnp.int32, (nb_pad, t), 0)
    blk_rowf = blk_row.astype(F32)
    q_pos = lax.broadcasted_iota(jnp.int32, (1, t), 1)
    lane = lax.broadcasted_iota(jnp.int32, (t, HEAD_DIM), 1)
    row = lax.broadcasted_iota(jnp.int32, (t, 1), 0)
    for it in range(s_len // t):
        rows = slice(it * t, (it + 1) * t)
        q = q_ref[rows, :]
        k = k_ref[rows, :]
        own = (it * t + q_pos) >> blk_shift
        gate = (_qk(km_hi, q) + _qk(km_lo, q))[:nb_pad, :]
        neg = jnp.full_like(gate, -jnp.inf)
        g = jnp.where(blk_row < own, gate, neg)
        visible = blk_row == own
        for _ in range(min(MOBA_TOPK, n_blocks)):
            mx = jnp.max(g, axis=0, keepdims=True)
            first = jnp.min(jnp.where(g == mx, blk_rowf, float(nb_pad)), axis=0, keepdims=True)
            pick = (blk_rowf == first) & (mx > -jnp.inf)
            visible = visible | pick
            g = jnp.where(pick, neg, g)
        bias_t = jnp.where(visible | (blk_row >= n_blocks), 0.0, MASKED_SCORE)
        bias_t = jnp.concatenate([bias_t, jnp.zeros((HEAD_DIM - nb_pad, t), F32)], axis=0)
        qaug_ref[rows, :] = jnp.concatenate([q, bias_t.T.astype(BF16)], axis=1)
        own_col = (it * t + row) >> blk_shift
        kaug_ref[rows, :] = jnp.concatenate([k, (lane == own_col).astype(BF16)], axis=1)

    def scores(i, j, c):
        q = qaug_ref[pl.ds(pl.multiple_of(i * t, t), t), :]
        k = kaug_ref[pl.ds(pl.multiple_of(j * t, t), t), :]
        return _qk(q, k) + bias_ref[(i == j).astype(jnp.int32)]

    bufs = _split_bufs(scratch)

    def finalize(i, acc):
        acc = acc[0]
        o_ref[i * t:(i + 1) * t, :] = (
            acc[:, :HEAD_DIM] / acc[:, HEAD_DIM:HEAD_DIM + 1]).astype(o_ref.dtype)

    _attn_pipeline(qi_ref, kj_ref, n_pairs, t, 1, scores, LOG2_E * slope * t, True,
                   vext_ref, bufs, finalize)


def _moba_attention(proj3, slopes, n_heads, q_blk, k_blk, v_blk, t):
    b, s, _ = proj3.shape
    n_blocks = s // MOBA_BLOCK
    assert n_blocks <= HEAD_DIM, "one lane per key block"
    qi, kj = _pair_tables(s // t)
    kern = functools.partial(_moba_kernel, t=t, n_pairs=qi.shape[0], n_blocks=n_blocks)
    smem = pl.BlockSpec(memory_space=pltpu.SMEM)
    return pl.pallas_call(
        kern,
        out_shape=jax.ShapeDtypeStruct((b, s, n_heads * HEAD_DIM), BF16),
        grid=(b, n_heads),
        in_specs=[smem, smem, smem,
                  _head_spec(s, q_blk), _head_spec(s, k_blk), _head_spec(s, v_blk)],
        out_specs=_head_spec(s, 0),
        scratch_shapes=[pltpu.VMEM((HEAD_DIM, HEAD_DIM), F32),
                        pltpu.VMEM((s, 2 * HEAD_DIM), BF16),
                        pltpu.VMEM((s, 2 * HEAD_DIM), BF16)] + _attn_scratch(s, t, 1),
        compiler_params=_params("parallel", "parallel"),
        name="moba_attention",
    )(slopes, qi, kj, proj3, proj3, proj3)


def _pack_cols(a, b):
    ua = lax.bitcast_convert_type(a, jnp.uint32)
    ub = lax.bitcast_convert_type(b, jnp.uint32)
    return (ua >> 16) | (ub & jnp.uint32(0xFFFF0000))


def _unpack_cols_f32(w):
    lo = lax.bitcast_convert_type(w << 16, F32)
    hi = lax.bitcast_convert_type(w & jnp.uint32(0xFFFF0000), F32)
    return lo, hi


def _unpack_cols(w):
    lo, hi = _unpack_cols_f32(w)
    return lo.astype(BF16), hi.astype(BF16)


def _lanes_from_columns(cols, dtype):
    tm = cols[0].shape[0]
    lane = lax.broadcasted_iota(jnp.int32, (tm, len(cols)), 1)
    out = jnp.zeros((tm, len(cols)), dtype)
    for r, c in enumerate(cols):
        out = jnp.where(lane == r, c.astype(dtype), out)
    return out


def _router_kernel(h_ref, w_ref, wr_ref, br_ref, xp_ref, ids_ref, gates_ref):
    x = h_ref[...]
    ms = jnp.mean(x * x, axis=-1, keepdims=True)
    n2 = (x * lax.rsqrt(ms + NORM_EPS)) * w_ref[...]
    hi = n2.astype(BF16)
    hi_f = hi.astype(F32)
    half = n2.shape[1] // 2
    xp_ref[...] = _pack_cols(hi_f[:, :half], hi_f[:, half:])

    lo = (n2 - hi_f).astype(BF16)
    wr = wr_ref[...]
    wr_hi = wr.astype(BF16)
    wr_lo = (wr - wr_hi.astype(F32)).astype(BF16)
    logits = (jnp.dot(hi, wr_hi, preferred_element_type=F32)
              + jnp.dot(lo, wr_hi, preferred_element_type=F32)
              + jnp.dot(hi, wr_lo, preferred_element_type=F32)) + br_ref[...]

    n_exp = logits.shape[1]
    eid = lax.broadcasted_iota(jnp.int32, logits.shape, 1).astype(F32)
    neg = jnp.full_like(logits, -jnp.inf)
    g = logits
    vals, idxs = [], []
    for _ in range(TOP_K):
        mx = jnp.max(g, axis=-1, keepdims=True)
        first = jnp.min(jnp.where(g == mx, eid, float(n_exp)), axis=-1, keepdims=True)
        vals.append(mx)
        idxs.append(first)
        g = jnp.where(eid == first, neg, g)
    exps = [jnp.exp(v - vals[0]) for v in vals]
    denom = exps[0] + exps[1] + exps[2] + exps[3]
    ids_ref[...] = _lanes_from_columns(idxs, jnp.int32)
    gates_ref[...] = _lanes_from_columns([e / denom for e in exps], F32)


def _router(h, norm_w, w_router, b_router, tm):
    t, d = h.shape
    n_exp = w_router.shape[1]
    return pl.pallas_call(
        _router_kernel,
        out_shape=(jax.ShapeDtypeStruct((t, d // 2), jnp.uint32),
                   jax.ShapeDtypeStruct((t, TOP_K), jnp.int32),
                   jax.ShapeDtypeStruct((t, TOP_K), F32)),
        grid=(t // tm,),
        in_specs=[pl.BlockSpec((tm, d), lambda i: (i, 0)),
                  pl.BlockSpec((1, d), lambda i: (0, 0)),
                  pl.BlockSpec((d, n_exp), lambda i: (0, 0)),
                  pl.BlockSpec((1, n_exp), lambda i: (0, 0))],
        out_specs=(pl.BlockSpec((tm, d // 2), lambda i: (i, 0)),
                   pl.BlockSpec((tm, TOP_K), lambda i: (i, 0)),
                   pl.BlockSpec((tm, TOP_K), lambda i: (i, 0))),
        compiler_params=_params("parallel"),
        name="rmsnorm2_router",
    )(h, norm_w.reshape(1, d), w_router, b_router.reshape(1, n_exp))


def _rank_kernel(ids_ref, rank_ref, counts_ref, carry_ref, *, n_exp):
    i = pl.program_id(0)

    @pl.when(i == 0)
    def _():
        carry_ref[...] = jnp.zeros_like(carry_ref)

    ids = ids_ref[...]
    tm = ids.shape[0]
    eid = lax.broadcasted_iota(jnp.int32, (tm, n_exp), 1)
    onehots = [(eid == ids[:, r:r + 1]).astype(F32) for r in range(TOP_K)]
    total = onehots[0] + onehots[1] + onehots[2] + onehots[3]
    row = lax.broadcasted_iota(jnp.int32, (tm, tm), 0)
    col = lax.broadcasted_iota(jnp.int32, (tm, tm), 1)
    strict_lower = (col < row).astype(BF16)
    before = jnp.dot(strict_lower, total.astype(BF16), preferred_element_type=F32)
    before = before + carry_ref[...]
    ranks = [jnp.sum(oh * before, axis=-1, keepdims=True) for oh in onehots]
    rank_ref[...] = _lanes_from_columns(ranks, jnp.int32)
    carry_ref[...] += jnp.sum(total, axis=0, keepdims=True)
    counts_ref[...] = carry_ref[...]


def _rank(ids, n_exp, tm):
    t = ids.shape[0]
    return pl.pallas_call(
        functools.partial(_rank_kernel, n_exp=n_exp),
        out_shape=(jax.ShapeDtypeStruct((t, TOP_K), jnp.int32),
                   jax.ShapeDtypeStruct((1, n_exp), F32)),
        grid=(t // tm,),
        in_specs=[pl.BlockSpec((tm, TOP_K), lambda i: (i, 0))],
        out_specs=(pl.BlockSpec((tm, TOP_K), lambda i: (i, 0)),
                   pl.BlockSpec((1, n_exp), lambda i: (0, 0))),
        scratch_shapes=[pltpu.VMEM((1, n_exp), F32)],
        compiler_params=_params("arbitrary"),
        name="expert_rank",
    )(ids)


def _dispatch_kernel(fill_start_ref, fill_n_ref, pos_ref, x_ref, buf_ref,
                     zeros_ref, sem, zsem, *, n_exp, tile_rows):
    i = pl.program_id(0)
    tm = x_ref.shape[0]
    zrows = zeros_ref.shape[0]
    sub = SUBLANES
    n_bits = (zrows // sub - 1).bit_length()

    def head_copy(e, r):
        a = fill_start_ref[e]
        n_head = jnp.minimum((-a) & (sub - 1), fill_n_ref[e])
        return r < n_head, pltpu.make_async_copy(
            zeros_ref.at[pl.ds(0, 1), :], buf_ref.at[pl.ds(a + r, 1), :], zsem)

    def body_copy(e, bit):
        a = fill_start_ref[e]
        n_head = jnp.minimum((-a) & (sub - 1), fill_n_ref[e])
        groups = (fill_n_ref[e] - n_head) // sub
        size = sub << bit
        dst = pl.multiple_of(a + n_head + sub * (groups & ((1 << bit) - 1)), sub)
        return (groups & (1 << bit)) != 0, pltpu.make_async_copy(
            zeros_ref.at[pl.ds(0, size), :], buf_ref.at[pl.ds(dst, size), :], zsem)

    def fill_copies():
        for e in range(n_exp):
            for r in range(sub - 1):
                yield head_copy(e, r)
            for bit in range(n_bits):
                yield body_copy(e, bit)

    def tail_copy(c):
        dst = pl.multiple_of(c * zrows, zrows)
        return pltpu.make_async_copy(zeros_ref, buf_ref.at[pl.ds(dst, zrows), :], zsem)

    @pl.when(i == 0)
    def _():
        zeros_ref[...] = jnp.zeros(zeros_ref.shape, zeros_ref.dtype)
        first_tail = fill_start_ref[n_exp] // zrows
        n_chunks = buf_ref.shape[0] // zrows

        def tail_start(c, carry):
            tail_copy(c).start()
            return carry

        def tail_wait(c, carry):
            tail_copy(c).wait()
            return carry

        for live, cp in fill_copies():
            pl.when(live)(cp.start)
        lax.fori_loop(first_tail, n_chunks, tail_start, 0)
        for live, cp in fill_copies():
            pl.when(live)(cp.wait)
        lax.fori_loop(first_tail, n_chunks, tail_wait, 0)

    def start(tok, carry):
        src = x_ref.at[pl.ds(tok, 1), :]
        for slot in range(TOP_K):
            dst = buf_ref.at[pl.ds(pos_ref[tok * TOP_K + slot], 1), :]
            pltpu.make_async_copy(src, dst, sem).start()
        return carry

    lax.fori_loop(0, tm, start, 0, unroll=DMA_ISSUE_UNROLL)
    for _ in range(TOP_K):
        pltpu.make_async_copy(x_ref, buf_ref.at[pl.ds(0, tm), :], sem).wait()


def _dispatch(fill_start, fill_n, pos_flat, xp, n_rows, tm, tile_rows):
    t, w = xp.shape
    n_exp = fill_n.shape[0]
    zrows = tile_rows
    grid_spec = pltpu.PrefetchScalarGridSpec(
        num_scalar_prefetch=2,
        grid=(t // tm,),
        in_specs=[pl.BlockSpec((tm * TOP_K,), lambda i, fs, fn: (i,), memory_space=pltpu.SMEM),
                  pl.BlockSpec((tm, w), lambda i, fs, fn: (i, 0))],
        out_specs=pl.BlockSpec(memory_space=pl.ANY),
        scratch_shapes=[pltpu.VMEM((zrows, w), xp.dtype),
                        pltpu.SemaphoreType.DMA(()),
                        pltpu.SemaphoreType.DMA(())],
    )
    return pl.pallas_call(
        functools.partial(_dispatch_kernel, n_exp=n_exp, tile_rows=tile_rows),
        out_shape=jax.ShapeDtypeStruct((n_rows, w), xp.dtype),
        grid_spec=grid_spec,
        compiler_params=_params("arbitrary"),
        name="dispatch",
    )(fill_start, fill_n, pos_flat, xp)


def _deinterleave_matrix(n):
    r = lax.broadcasted_iota(jnp.int32, (n, n), 0)
    c = lax.broadcasted_iota(jnp.int32, (n, n), 1)
    src = jnp.where(c < n // 2, 2 * c, 2 * (c - n // 2) + 1)
    return (r == src).astype(BF16)


TILE_EMPTY, TILE_HALF, TILE_FULL = 0, 1, 2


def _for_tile_fill(fill, o_ref, compute):
    tm = o_ref.shape[0]
    pl.when(fill == TILE_FULL)(functools.partial(compute, tm))
    pl.when(fill == TILE_HALF)(functools.partial(compute, tm // 2))

    @pl.when(fill == TILE_EMPTY)
    def _():
        o_ref[...] = jnp.zeros_like(o_ref)


def _stage_expert_weights(w_hbm, stage_ref, sem, tabs, convert):
    te_ref, tf_ref, ne_ref, nw_ref = tabs
    c = pl.program_id(0)
    i = pl.program_id(1)
    width = stage_ref.shape[1]

    def block_copy(e, chunk):
        col = pl.multiple_of(chunk * width, width)
        return pltpu.make_async_copy(w_hbm.at[e, :, pl.ds(col, width)], stage_ref, sem)

    @pl.when((c == 0) & (i == 0))
    def _():
        block_copy(te_ref[0], 0).start()

    @pl.when(tf_ref[i] == 1)
    def _():
        block_copy(te_ref[i], c).wait()
        convert()
        next_chunk = c + nw_ref[i]

        @pl.when(next_chunk < pl.num_programs(0))
        def _():
            block_copy(ne_ref[i], next_chunk).start()


def _gateup_kernel(te_ref, tv_ref, tf_ref, tr_ref, ne_ref, nw_ref, x_ref, w_hbm, bg_ref, bu_ref,
                   o_ref, stage_ref, wg_ref, wu_ref, sem):
    del tr_ref
    i = pl.program_id(1)
    grp = V7X_MXU_DIM
    half = grp // 2

    def convert():
        perm = _deinterleave_matrix(grp)
        for g in range(stage_ref.shape[1] // grp):
            w = stage_ref[:, g * grp:(g + 1) * grp].astype(BF16)
            wp = jnp.dot(w, perm, preferred_element_type=F32).astype(BF16)
            wg_ref[:, g * half:(g + 1) * half] = wp[:, :half]
            wu_ref[:, g * half:(g + 1) * half] = wp[:, half:]

    _stage_expert_weights(w_hbm, stage_ref, sem, (te_ref, tf_ref, ne_ref, nw_ref), convert)

    def compute(rows):
        xa, xb = _unpack_cols(x_ref[:rows, :])
        k_half = xa.shape[1]

        def proj(w_s, b_ref):
            return (jnp.dot(xa, w_s[:k_half, :], preferred_element_type=F32)
                    + jnp.dot(xb, w_s[k_half:, :], preferred_element_type=F32)
                    + b_ref[...])

        g = jnp.minimum(proj(wg_ref, bg_ref), SWIGLU_LIMIT)
        u = jnp.clip(proj(wu_ref, bu_ref), -SWIGLU_LIMIT, SWIGLU_LIMIT)
        sig = 1.0 / (1.0 + jnp.exp(-(g * SWIGLU_ALPHA)))
        o_ref[:rows, :] = ((u + 1.0) * (g * sig)).astype(o_ref.dtype)
        if rows < o_ref.shape[0]:
            o_ref[rows:, :] = jnp.zeros((o_ref.shape[0] - rows, o_ref.shape[1]), o_ref.dtype)

    _for_tile_fill(tv_ref[i], o_ref, compute)


def _gateup(tabs, xs, w_gu, bg, bu, tm, fc):
    rows, k_half = xs.shape
    n_exp, d, f2 = w_gu.shape
    f = f2 // 2
    n_tiles = rows // tm
    n_chunks = f // fc

    def b_map(c, i, te, tv, tf, tr, ne, nw):
        return (te[i], 0, c)

    grid_spec = pltpu.PrefetchScalarGridSpec(
        num_scalar_prefetch=6,
        grid=(n_chunks, n_tiles),
        in_specs=[pl.BlockSpec((tm, k_half), lambda c, i, te, tv, tf, tr, ne, nw: (tr[i], 0)),
                  pl.BlockSpec(memory_space=pl.ANY),
                  pl.BlockSpec((None, 1, fc), b_map),
                  pl.BlockSpec((None, 1, fc), b_map)],
        out_specs=pl.BlockSpec((tm, fc), lambda c, i, te, tv, tf, tr, ne, nw: (i, c)),
        scratch_shapes=[pltpu.VMEM((d, 2 * fc), F32),
                        pltpu.VMEM((d, fc), BF16), pltpu.VMEM((d, fc), BF16),
                        pltpu.SemaphoreType.DMA(())],
    )
    return pl.pallas_call(
        _gateup_kernel,
        out_shape=jax.ShapeDtypeStruct((rows, f), BF16),
        grid_spec=grid_spec,
        compiler_params=_params("arbitrary", "arbitrary"),
        name="expert_gate_up",
    )(*tabs, xs, w_gu, bg, bu)


def _down_kernel(te_ref, tv_ref, tf_ref, ne_ref, nw_ref, a_ref, w_hbm, b_ref, o_ref,
                 stage_ref, wb_ref, sem):
    i = pl.program_id(1)

    def convert():
        wb_ref[...] = stage_ref[...].astype(BF16)

    _stage_expert_weights(w_hbm, stage_ref, sem, (te_ref, tf_ref, ne_ref, nw_ref), convert)

    def compute(rows):
        o = jnp.dot(a_ref[:rows, :], wb_ref[...], preferred_element_type=F32) + b_ref[...]
        ob = o.astype(BF16).astype(F32)
        half = ob.shape[1] // 2
        o_ref[:rows, :] = _pack_cols(ob[:, :half], ob[:, half:])
        if rows < o_ref.shape[0]:
            o_ref[rows:, :] = jnp.zeros((o_ref.shape[0] - rows, o_ref.shape[1]), o_ref.dtype)

    _for_tile_fill(tv_ref[i], o_ref, compute)


def _down(tabs, act, wd, bd, tm, tn):
    rows, f = act.shape
    n_exp, _, d = wd.shape
    n_chunks = d // tn

    grid_spec = pltpu.PrefetchScalarGridSpec(
        num_scalar_prefetch=5,
        grid=(n_chunks, rows // tm),
        in_specs=[pl.BlockSpec((tm, f), lambda c, i, te, tv, tf, ne, nw: (i, 0)),
                  pl.BlockSpec(memory_space=pl.ANY),
                  pl.BlockSpec((None, 1, tn), lambda c, i, te, tv, tf, ne, nw: (te[i], 0, c))],
        out_specs=pl.BlockSpec((tm, tn // 2), lambda c, i, te, tv, tf, ne, nw: (i, c)),
        scratch_shapes=[pltpu.VMEM((f, tn), F32), pltpu.VMEM((f, tn), BF16),
                        pltpu.SemaphoreType.DMA(())],
    )
    return pl.pallas_call(
        _down_kernel,
        out_shape=jax.ShapeDtypeStruct((rows, d // 2), jnp.uint32),
        grid_spec=grid_spec,
        compiler_params=_params("arbitrary", "arbitrary"),
        name="expert_down",
    )(*tabs, act, wd, bd)


def _combine_kernel(pos_ref, gates_ref, h_ref, w_ref, y_ref, o_ref, rows_ref, sem, *,
                    packed_chunk):
    tm = h_ref.shape[0]

    def start(tok, carry):
        for slot in range(TOP_K):
            src = y_ref.at[pl.ds(pos_ref[tok * TOP_K + slot], 1), :]
            pltpu.make_async_copy(src, rows_ref.at[slot, pl.ds(tok, 1), :], sem).start()
        return carry

    lax.fori_loop(0, tm, start, 0, unroll=DMA_ISSUE_UNROLL)
    for r in range(TOP_K):
        pltpu.make_async_copy(y_ref.at[pl.ds(0, tm), :], rows_ref.at[r], sem).wait()

    gates = gates_ref[...]
    lo, hi = _unpack_cols_f32(rows_ref[0])
    moe_lo = gates[:, 0:1] * lo
    moe_hi = gates[:, 0:1] * hi
    for r in range(1, TOP_K):
        lo, hi = _unpack_cols_f32(rows_ref[r])
        moe_lo += gates[:, r:r + 1] * lo
        moe_hi += gates[:, r:r + 1] * hi
    pieces = []
    for c in range(moe_lo.shape[1] // packed_chunk):
        cols = slice(c * packed_chunk, (c + 1) * packed_chunk)
        pieces += [moe_lo[:, cols], moe_hi[:, cols]]
    x = h_ref[...] + jnp.concatenate(pieces, axis=1)
    ms = jnp.mean(x * x, axis=-1, keepdims=True)
    o_ref[...] = (x * lax.rsqrt(ms + NORM_EPS)) * w_ref[...]


def _combine(pos_flat, gates, h, norm_w, ys, tm, down_cols):
    t, d = h.shape
    return pl.pallas_call(
        functools.partial(_combine_kernel, packed_chunk=down_cols // 2),
        out_shape=jax.ShapeDtypeStruct((t, d), F32),
        grid=(t // tm,),
        in_specs=[pl.BlockSpec((tm * TOP_K,), lambda i: (i,), memory_space=pltpu.SMEM),
                  pl.BlockSpec((tm, TOP_K), lambda i: (i, 0)),
                  pl.BlockSpec((tm, d), lambda i: (i, 0)),
                  pl.BlockSpec((1, d), lambda i: (0, 0)),
                  pl.BlockSpec(memory_space=pl.ANY)],
        out_specs=pl.BlockSpec((tm, d), lambda i: (i, 0)),
        scratch_shapes=[pltpu.VMEM((TOP_K, tm, d // 2), jnp.uint32),
                        pltpu.SemaphoreType.DMA(())],
        compiler_params=_params("arbitrary"),
        name="combine_final_norm",
    )(pos_flat, gates, h, norm_w.reshape(1, d), ys)


def _tiles(t, s, d, f):
    return dict(
        norm_rows=min(256, t),
        mm_rows=min(1024, t),
        mm_cols=min(1024, d),
        attn_tile=min(512, s),
        router_rows=min(256, t),
        rank_rows=min(512, t),
        dispatch_rows=min(256, t),
        expert_rows=min(512, t),
        ff_chunk=min(512, f),
        down_cols=min(2048, d),
        combine_rows=min(256, t),
    )


def _alibi_slopes(n):
    return jnp.exp2(-ALIBI_MAX_BIAS * jnp.arange(1, n + 1, dtype=F32) / n)


def _routing_tables(ids, rank, counts, tm, n_tiles):
    n_exp = counts.shape[0]
    i32 = jnp.int32
    padded = ((counts + tm - 1) // tm) * tm
    ends = jnp.cumsum(padded)
    starts = ends - padded
    onehot = ids[..., None] == jnp.arange(n_exp, dtype=i32)
    pos = rank + jnp.sum(jnp.where(onehot, starts, 0), axis=-1)
    tile_start = jnp.arange(n_tiles, dtype=i32) * tm
    tile_valid = (tile_start < ends[-1]).astype(i32)
    tile_expert = jnp.minimum(jnp.sum((tile_start[:, None] >= ends[None, :]).astype(i32), axis=1),
                              n_exp - 1)
    prev_expert = jnp.concatenate([jnp.full((1,), -1, i32), tile_expert[:-1]])
    tile_first = tile_valid * (tile_expert != prev_expert).astype(i32)
    of_expert = tile_expert[:, None] == jnp.arange(n_exp, dtype=i32)[None, :]
    rows_end = jnp.sum(jnp.where(of_expert, (starts + counts)[None, :], 0), axis=1)
    tile_rows = jnp.clip(rows_end - tile_start, 0, tm)
    tile_fill = tile_valid * jnp.where(tile_rows <= tm // 2, TILE_HALF, TILE_FULL).astype(i32)
    tile_idx = jnp.arange(n_tiles, dtype=i32)
    later_first = (tile_idx[None, :] > tile_idx[:, None]) & (tile_first[None, :] == 1)
    next_first = jnp.min(jnp.where(later_first, tile_idx[None, :], n_tiles), axis=1)
    next_wraps = (next_first == n_tiles).astype(i32)
    next_expert = jnp.sum(jnp.where(tile_idx[None, :] == (next_first % n_tiles)[:, None],
                                    tile_expert[None, :], 0), axis=1).astype(i32)
    tile_row = jnp.minimum(jnp.arange(n_tiles, dtype=i32), ends[-1] // tm - 1)
    fill_start = jnp.concatenate([starts + counts, ends[-1:]]).astype(i32)
    fill_n = (padded - counts).astype(i32)
    tiles = dict(expert=tile_expert, fill=tile_fill, first=tile_first, row=tile_row,
                 next_expert=next_expert, next_wraps=next_wraps)
    return pos.reshape(-1), tiles, fill_start, fill_n


def _layer(h, l, norm1_w, w_in, lam_q1, lam_k1, lam_q2, lam_k2, subln_w, w_out, norm2_w,
           w_router, b_router, w_gate_up, b_gate_up, w_down, b_down, out_norm_w):
    b, s, d = h.shape
    t = b * s
    n_exp = w_router.shape[-1]
    f = w_down.shape[-2]
    n_heads = d // 2 // HEAD_DIM
    width = n_heads * HEAD_DIM
    tl = _tiles(t, s, d, f)
    lambda_init = 0.8 - 0.6 * math.exp(-0.3 * l)
    x2 = h.reshape(t, d)

    n1 = _rmsnorm(x2, norm1_w[l], tl["norm_rows"])
    n_cols = w_in.shape[-1]
    col_scale = jnp.ones((n_cols,), F32)
    col_scale = col_scale.at[:width].set(LOG2_E * DIFF_QK_DIM ** -0.5)
    col_scale = col_scale.at[3 * width:4 * width].set(LOG2_E * HEAD_DIM ** -0.5)
    proj = _matmul(n1, w_in[l].astype(BF16), col_scale.reshape(1, n_cols), tl["mm_rows"],
                   tl["mm_cols"], BF16)
    proj3 = proj.reshape(b, s, proj.shape[1])
    slopes = _alibi_slopes(n_heads)
    lam4 = jnp.stack([lam_q1[l], lam_k1[l], lam_q2[l], lam_k2[l]]).astype(F32)
    blocks = width // HEAD_DIM
    y_a = _diff_attention(proj3, slopes, lam4, subln_w[l].reshape(1, HEAD_DIM).astype(F32),
                          n_heads, 0, blocks, 2 * blocks, tl["attn_tile"], lambda_init)
    y_b = _moba_attention(proj3, slopes, n_heads, 3 * blocks, 4 * blocks, 5 * blocks,
                          tl["attn_tile"])
    h1 = _outproj(y_a.reshape(t, width), y_b.reshape(t, width), w_out[l].astype(BF16), x2,
                  tl["mm_rows"], tl["mm_cols"])

    xp, ids, gates = _router(h1, norm2_w[l], w_router[l], b_router[l], tl["router_rows"])
    rank, counts = _rank(ids, n_exp, tl["rank_rows"])
    tm = tl["expert_rows"]
    n_tiles = (t * TOP_K) // tm + n_exp
    pos_flat, tiles, fill_start, fill_n = _routing_tables(
        ids, rank, counts.reshape(n_exp).astype(jnp.int32), tm, n_tiles)

    xs = _dispatch(fill_start, fill_n, pos_flat, xp, n_tiles * tm, tl["dispatch_rows"], tm)
    bgu = b_gate_up[l].reshape(n_exp, 1, f, 2)
    gateup_tabs = (tiles["expert"], tiles["fill"], tiles["first"], tiles["row"],
                   tiles["next_expert"], tiles["next_wraps"])
    act = _gateup(gateup_tabs, xs, w_gate_up[l], bgu[..., 0], bgu[..., 1], tm, tl["ff_chunk"])
    down_tabs = (tiles["expert"], tiles["fill"], tiles["first"],
                 tiles["next_expert"], tiles["next_wraps"])
    ys = _down(down_tabs, act, w_down[l], b_down[l].reshape(n_exp, 1, d), tm, tl["down_cols"])
    return _combine(pos_flat, gates, h1, out_norm_w, ys, tl["combine_rows"],
                    tl["down_cols"]).reshape(b, s, d)


def kernel(x, norm1_w, w_in, lam_q1, lam_k1, lam_q2, lam_k2, subln_w, w_out, norm2_w,
           w_router, b_router, w_gate_up, b_gate_up, w_down, b_down, final_norm_w):
    depth = w_in.shape[0]
    assert depth == 1, "the fused combine + final norm stage assumes a single layer"
    return _layer(x, 0, norm1_w, w_in, lam_q1, lam_k1, lam_q2, lam_k2, subln_w, w_out,
                  norm2_w, w_router, b_router, w_gate_up, b_gate_up, w_down, b_down,
                  final_norm_w)
```

```python
import functools
import math

import jax
import jax.numpy as jnp
from jax import lax
from jax.experimental import pallas as pl
from jax.experimental.pallas import tpu as pltpu

F32 = jnp.float32
BF16 = jnp.bfloat16

HEAD_DIM = 128
DIFF_QK_DIM = HEAD_DIM // 2
MOBA_BLOCK = 256
MOBA_TOPK = 3
TOP_K = 4
SWIGLU_LIMIT = 7.0
SWIGLU_ALPHA = 1.702
NORM_EPS = 1e-5
ALIBI_MAX_BIAS = 8.0

V7X_VMEM_BYTES = 64 * 1024 * 1024
VMEM_LIMIT_BYTES = V7X_VMEM_BYTES - 8 * 1024 * 1024
V7X_MXU_DIM = 256
SUBLANES = 8
LANES = 128
DMA_ISSUE_UNROLL = 8


def _params(*semantics):
    return pltpu.CompilerParams(dimension_semantics=semantics,
                                vmem_limit_bytes=VMEM_LIMIT_BYTES)


def _rmsnorm_kernel(x_ref, w_ref, o_ref):
    x = x_ref[...]
    ms = jnp.mean(x * x, axis=-1, keepdims=True)
    o_ref[...] = ((x * lax.rsqrt(ms + NORM_EPS)) * w_ref[...]).astype(o_ref.dtype)


def _rmsnorm(x, w, tm):
    t, d = x.shape
    return pl.pallas_call(
        _rmsnorm_kernel,
        out_shape=jax.ShapeDtypeStruct((t, d), BF16),
        grid=(t // tm,),
        in_specs=[pl.BlockSpec((tm, d), lambda i: (i, 0)),
                  pl.BlockSpec((1, d), lambda i: (0, 0))],
        out_specs=pl.BlockSpec((tm, d), lambda i: (i, 0)),
        compiler_params=_params("parallel"),
        name="rmsnorm1",
    )(x, w.reshape(1, d))


def _matmul_kernel(a_ref, b_ref, cs_ref, o_ref):
    acc = jnp.dot(a_ref[...], b_ref[...], preferred_element_type=F32)
    o_ref[...] = (acc * cs_ref[...]).astype(o_ref.dtype)


def _matmul(a, b, col_scale, tm, tn, out_dtype):
    m, k = a.shape
    _, n = b.shape
    return pl.pallas_call(
        _matmul_kernel,
        out_shape=jax.ShapeDtypeStruct((m, n), out_dtype),
        grid=(m // tm, n // tn),
        in_specs=[pl.BlockSpec((tm, k), lambda i, j: (i, 0)),
                  pl.BlockSpec((k, tn), lambda i, j: (0, j)),
                  pl.BlockSpec((1, tn), lambda i, j: (0, j))],
        out_specs=pl.BlockSpec((tm, tn), lambda i, j: (i, j)),
        compiler_params=_params("parallel", "parallel"),
        name="in_proj",
    )(a, b, col_scale)


def _outproj_kernel(ya_ref, yb_ref, w_ref, x_ref, o_ref):
    ka = ya_ref.shape[1]
    acc = jnp.dot(ya_ref[...], w_ref[:ka, :], preferred_element_type=F32)
    acc += jnp.dot(yb_ref[...], w_ref[ka:, :], preferred_element_type=F32)
    o_ref[...] = x_ref[...] + acc


def _outproj(ya, yb, w, x, tm, tn):
    m, ka = ya.shape
    kb = yb.shape[1]
    n = w.shape[1]
    return pl.pallas_call(
        _outproj_kernel,
        out_shape=jax.ShapeDtypeStruct((m, n), F32),
        grid=(m // tm, n // tn),
        in_specs=[pl.BlockSpec((tm, ka), lambda i, j: (i, 0)),
                  pl.BlockSpec((tm, kb), lambda i, j: (i, 0)),
                  pl.BlockSpec((ka + kb, tn), lambda i, j: (0, j)),
                  pl.BlockSpec((tm, tn), lambda i, j: (i, j))],
        out_specs=pl.BlockSpec((tm, tn), lambda i, j: (i, j)),
        compiler_params=_params("parallel", "parallel"),
        name="out_proj",
    )(ya, yb, w, x)


RUNNING_MAX_INIT = -1e30
LOG2_E = 1.4426950408889634


def _qk(q, k):
    return lax.dot_general(q, k, (((1,), (1,)), ((), ())), preferred_element_type=F32)


def _pair_tables(n_tiles):
    qi = [i for i in range(n_tiles) for _ in range(i + 1)]
    kj = [j for i in range(n_tiles) for j in range(i + 1)]
    return jnp.asarray(qi, jnp.int32), jnp.asarray(kj, jnp.int32)


def _fill_vext(vext_ref, v_ref):
    vext_ref[:, :HEAD_DIM] = v_ref[...]
    vext_ref[:, HEAD_DIM:] = jnp.ones((v_ref.shape[0], HEAD_DIM), vext_ref.dtype)


def _fill_tile_bias(bias_ref, col_bias_scale, t):
    row = lax.broadcasted_iota(jnp.int32, (t, t), 0)
    col = lax.broadcasted_iota(jnp.int32, (t, t), 1)
    cb = col_bias_scale * col.astype(F32)
    bias_ref[0] = cb
    bias_ref[1] = jnp.where(col <= row, cb, -jnp.inf)


def _widen(stat, n):
    return stat if stat.shape[1] == 1 else jnp.tile(stat, (1, n // stat.shape[1]))


def _attn_scratch(s_len, t, n_comp):
    stat = LANES if n_comp == 1 else 1
    return [pltpu.VMEM((s_len, 2 * HEAD_DIM), BF16),
            pltpu.VMEM((2, t, t), F32),
            pltpu.VMEM((n_comp, t, t), F32),
            pltpu.VMEM((n_comp, t, t), F32),
            pltpu.VMEM((n_comp, t, t), BF16),
            pltpu.VMEM((n_comp, t, t), BF16),
            pltpu.VMEM((n_comp, t, stat), F32),
            pltpu.VMEM((n_comp, t, stat), F32),
            pltpu.VMEM((n_comp, t, stat), F32),
            pltpu.VMEM((n_comp, t, 2 * HEAD_DIM), F32),
            pltpu.VMEM((s_len // t + 1, n_comp, t, 2 * HEAD_DIM), F32)]


def _attn_pipeline(qi_ref, kj_ref, n_pairs, t, n_comp, scores, shift_step, rows_can_be_empty,
                   vext_ref, bufs, finalize):
    s_bufs, p_bufs, a_bufs, m_ref, acc_ref, done_ref = bufs
    for buf in s_bufs + p_bufs:
        buf[...] = jnp.zeros(buf.shape, buf.dtype)
    for buf in a_bufs:
        buf[...] = jnp.ones(buf.shape, buf.dtype)
    m_ref[...] = jnp.full(m_ref.shape, RUNNING_MAX_INIT, F32)
    acc_ref[...] = jnp.zeros(acc_ref.shape, F32)

    def pair(n):
        nc = jnp.clip(n, 0, n_pairs - 1)
        return qi_ref[nc], kj_ref[nc], (n >= 0) & (n < n_pairs)

    def stage_scores(n, s_buf):
        i, j, _ = pair(n)
        for c in range(n_comp):
            s_buf[c] = scores(i, j, c)

    def stage_softmax(n, s_buf, p_buf, a_buf):
        _, j, _ = pair(n)
        first = j == 0
        max_cap = jnp.where(first, RUNNING_MAX_INIT, jnp.inf)
        keep = jnp.where(first, 0.0, 1.0)
        for c in range(n_comp):
            m_prev = jnp.minimum(m_ref[c], max_cap)
            row_max = jnp.broadcast_to(jnp.max(s_buf[c], axis=-1, keepdims=True), m_prev.shape)
            m_cur = jnp.maximum(m_prev, row_max)
            m_ref[c] = m_cur - shift_step
            p_buf[c] = jnp.exp2(s_buf[c] - _widen(m_cur, t)).astype(p_buf.dtype)
            rescale = jnp.exp2(m_prev - m_cur)
            a_buf[c] = rescale * keep if rows_can_be_empty else rescale

    n_tiles = done_ref.shape[0] - 1

    def stage_values(n, p_buf, a_buf):
        i, j, active = pair(n)
        vext = vext_ref[pl.ds(pl.multiple_of(j * t, t), t), :]
        slot = jnp.where(active, i, n_tiles)
        for c in range(n_comp):
            acc = (_widen(a_buf[c], 2 * HEAD_DIM) * acc_ref[c]
                   + jnp.dot(p_buf[c], vext, preferred_element_type=F32))
            acc_ref[c] = acc
            done_ref[slot, c] = acc

    def body(step, carry):
        for parity in range(2):
            n = 2 * step + parity
            stage_values(n - 2, p_bufs[parity], a_bufs[parity])
            stage_scores(n, s_bufs[parity])
            stage_softmax(n - 1, s_bufs[1 - parity], p_bufs[1 - parity], a_bufs[1 - parity])
        return carry

    lax.fori_loop(0, (n_pairs + 3) // 2, body, 0)
    for i in range(n_tiles):
        finalize(i, done_ref[i])


def _split_bufs(scratch):
    s0, s1, p0, p1, a0, a1, m_ref, acc_ref, done_ref = scratch
    return (s0, s1), (p0, p1), (a0, a1), m_ref, acc_ref, done_ref


def _diff_attn_kernel(slopes_ref, qi_ref, kj_ref, lam_ref, subw_ref, q_ref, k_ref, v_ref, o_ref,
                      qc_ref, vext_ref, bias_ref, *scratch, t, n_pairs, lambda_init):
    slope = slopes_ref[pl.program_id(1)]
    dq = DIFF_QK_DIM
    _fill_vext(vext_ref, v_ref)
    _fill_tile_bias(bias_ref, LOG2_E * slope, t)

    q_all = q_ref[...]
    lane = lax.broadcasted_iota(jnp.int32, q_all.shape, 1)
    qc_ref[0] = jnp.where(lane < dq, q_all, jnp.zeros_like(q_all))
    qc_ref[1] = jnp.where(lane >= dq, q_all, jnp.zeros_like(q_all))

    def scores(i, j, c):
        q = qc_ref[c, pl.ds(pl.multiple_of(i * t, t), t), :]
        k = k_ref[pl.ds(pl.multiple_of(j * t, t), t), :]
        return _qk(q, k) + bias_ref[(i == j).astype(jnp.int32)]

    bufs = _split_bufs(scratch)

    def finalize(i, acc):
        lam_v = lam_ref[...]
        lam = (jnp.exp(jnp.sum(lam_v[0:1, :] * lam_v[1:2, :], axis=-1, keepdims=True))
               - jnp.exp(jnp.sum(lam_v[2:3, :] * lam_v[3:4, :], axis=-1, keepdims=True))
               + lambda_init)
        a1 = acc[0]
        a2 = acc[1]
        o = (a1[:, :HEAD_DIM] / a1[:, HEAD_DIM:HEAD_DIM + 1]
             - lam * (a2[:, :HEAD_DIM] / a2[:, HEAD_DIM:HEAD_DIM + 1]))
        ms = jnp.mean(o * o, axis=-1, keepdims=True)
        o = (o * lax.rsqrt(ms + NORM_EPS)) * subw_ref[...]
        o_ref[i * t:(i + 1) * t, :] = (o * (1.0 - lambda_init)).astype(o_ref.dtype)

    _attn_pipeline(qi_ref, kj_ref, n_pairs, t, 2, scores, LOG2_E * slope * t, False,
                   vext_ref, bufs, finalize)


def _head_spec(s, blk):
    return pl.BlockSpec((None, s, HEAD_DIM), lambda bi, h: (bi, 0, blk + h))


def _diff_attention(proj3, slopes, lam4, subw, n_heads, q_blk, k_blk, v_blk, t, lambda_init):
    b, s, _ = proj3.shape
    qi, kj = _pair_tables(s // t)
    kern = functools.partial(_diff_attn_kernel, t=t, n_pairs=qi.shape[0], lambda_init=lambda_init)
    smem = pl.BlockSpec(memory_space=pltpu.SMEM)
    return pl.pallas_call(
        kern,
        out_shape=jax.ShapeDtypeStruct((b, s, n_heads * HEAD_DIM), BF16),
        grid=(b, n_heads),
        in_specs=[smem, smem, smem,
                  pl.BlockSpec((4, DIFF_QK_DIM), lambda bi, h: (0, 0)),
                  pl.BlockSpec((1, HEAD_DIM), lambda bi, h: (0, 0)),
                  _head_spec(s, q_blk), _head_spec(s, k_blk), _head_spec(s, v_blk)],
        out_specs=_head_spec(s, 0),
        scratch_shapes=[pltpu.VMEM((2, s, HEAD_DIM), BF16)] + _attn_scratch(s, t, 2),
        compiler_params=_params("parallel", "parallel"),
        name="diff_attention",
    )(slopes, qi, kj, lam4, subw, proj3, proj3, proj3)


MASKED_SCORE = -(2.0 ** 127)


def _moba_kernel(slopes_ref, qi_ref, kj_ref, q_ref, k_ref, v_ref, o_ref,
                 kmean_ref, qaug_ref, kaug_ref, vext_ref, bias_ref, *scratch,
                 t, n_pairs, n_blocks):
    slope = slopes_ref[pl.program_id(1)]
    blk = MOBA_BLOCK
    blk_shift = blk.bit_length() - 1
    s_len = q_ref.shape[0]

    _fill_vext(vext_ref, v_ref)
    _fill_tile_bias(bias_ref, LOG2_E * slope, t)

    kmean_ref[...] = jnp.zeros(kmean_ref.shape, F32)
    for n in range(n_blocks):
        kb = k_ref[n * blk:(n + 1) * blk, :].astype(F32)
        kmean_ref[n:n + 1, :] = jnp.mean(kb, axis=0, keepdims=True)
    km = kmean_ref[...]
    km_hi = km.astype(BF16)
    km_lo = (km - km_hi.astype(F32)).astype(BF16)

    nb_pad = -(-n_blocks // SUBLANES) * SUBLANES
    blk_row = lax.broadcasted_iota(jnp.int32, (nb_pad, t), 0)
    blk_rowf = blk_row.astype(F32)
    q_pos = lax.broadcasted_iota(jnp.int32, (1, t), 1)
    lane = lax.broadcasted_iota(jnp.int32, (t, HEAD_DIM), 1)
    row = lax.broadcasted_iota(jnp.int32, (t, 1), 0)
    for it in range(s_len // t):
        rows = slice(it * t, (it + 1) * t)
        q = q_ref[rows, :]
        k = k_ref[rows, :]
        own = (it * t + q_pos) >> blk_shift
        gate = (_qk(km_hi, q) + _qk(km_lo, q))[:nb_pad, :]
        neg = jnp.full_like(gate, -jnp.inf)
        g = jnp.where(blk_row < own, gate, neg)
        visible = blk_row == own
        for _ in range(min(MOBA_TOPK, n_blocks)):
            mx = jnp.max(g, axis=0, keepdims=True)
            first = jnp.min(jnp.where(g == mx, blk_rowf, float(nb_pad)), axis=0, keepdims=True)
            pick = (blk_rowf == first) & (mx > -jnp.inf)
            visible = visible | pick
            g = jnp.where(pick, neg, g)
        bias_t = jnp.where(visible | (blk_row >= n_blocks), 0.0, MASKED_SCORE)
        bias_t = jnp.concatenate([bias_t, jnp.zeros((HEAD_DIM - nb_pad, t), F32)], axis=0)
        qaug_ref[rows, :] = jnp.concatenate([q, bias_t.T.astype(BF16)], axis=1)
        own_col = (it * t + row) >> blk_shift
        kaug_ref[rows, :] = jnp.concatenate([k, (lane == own_col).astype(BF16)], axis=1)

    def scores(i, j, c):
        q = qaug_ref[pl.ds(pl.multiple_of(i * t, t), t), :]
        k = kaug_ref[pl.ds(pl.multiple_of(j * t, t), t), :]
        return _qk(q, k) + bias_ref[(i == j).astype(jnp.int32)]

    bufs = _split_bufs(scratch)

    def finalize(i, acc):
        acc = acc[0]
        o_ref[i * t:(i + 1) * t, :] = (
            acc[:, :HEAD_DIM] / acc[:, HEAD_DIM:HEAD_DIM + 1]).astype(o_ref.dtype)

    _attn_pipeline(qi_ref, kj_ref, n_pairs, t, 1, scores, LOG2_E * slope * t, True,
                   vext_ref, bufs, finalize)


def _moba_attention(proj3, slopes, n_heads, q_blk, k_blk, v_blk, t):
    b, s, _ = proj3.shape
    n_blocks = s // MOBA_BLOCK
    assert n_blocks <= HEAD_DIM, "one lane per key block"
    qi, kj = _pair_tables(s // t)
    kern = functools.partial(_moba_kernel, t=t, n_pairs=qi.shape[0], n_blocks=n_blocks)
    smem = pl.BlockSpec(memory_space=pltpu.SMEM)
    return pl.pallas_call(
        kern,
        out_shape=jax.ShapeDtypeStruct((b, s, n_heads * HEAD_DIM), BF16),
        grid=(b, n_heads),
        in_specs=[smem, smem, smem,
                  _head_spec(s, q_blk), _head_spec(s, k_blk), _head_spec(s, v_blk)],
        out_specs=_head_spec(s, 0),
        scratch_shapes=[pltpu.VMEM((HEAD_DIM, HEAD_DIM), F32),
                        pltpu.VMEM((s, 2 * HEAD_DIM), BF16),
                        pltpu.VMEM((s, 2 * HEAD_DIM), BF16)] + _attn_scratch(s, t, 1),
        compiler_params=_params("parallel", "parallel"),
        name="moba_attention",
    )(slopes, qi, kj, proj3, proj3, proj3)


def _pack_cols(a, b):
    ua = lax.bitcast_convert_type(a, jnp.uint32)
    ub = lax.bitcast_convert_type(b, jnp.uint32)
    return (ua >> 16) | (ub & jnp.uint32(0xFFFF0000))


def _unpack_cols_f32(w):
    lo = lax.bitcast_convert_type(w << 16, F32)
    hi = lax.bitcast_convert_type(w & jnp.uint32(0xFFFF0000), F32)
    return lo, hi


def _unpack_cols(w):
    lo, hi = _unpack_cols_f32(w)
    return lo.astype(BF16), hi.astype(BF16)


def _lanes_from_columns(cols, dtype):
    tm = cols[0].shape[0]
    lane = lax.broadcasted_iota(jnp.int32, (tm, len(cols)), 1)
    out = jnp.zeros((tm, len(cols)), dtype)
    for r, c in enumerate(cols):
        out = jnp.where(lane == r, c.astype(dtype), out)
    return out


def _router_kernel(h_ref, w_ref, wr_ref, br_ref, xp_ref, ids_ref, gates_ref):
    x = h_ref[...]
    ms = jnp.mean(x * x, axis=-1, keepdims=True)
    n2 = (x * lax.rsqrt(ms + NORM_EPS)) * w_ref[...]
    hi = n2.astype(BF16)
    hi_f = hi.astype(F32)
    half = n2.shape[1] // 2
    xp_ref[...] = _pack_cols(hi_f[:, :half], hi_f[:, half:])

    lo = (n2 - hi_f).astype(BF16)
    wr = wr_ref[...]
    wr_hi = wr.astype(BF16)
    wr_lo = (wr - wr_hi.astype(F32)).astype(BF16)
    logits = (jnp.dot(hi, wr_hi, preferred_element_type=F32)
              + jnp.dot(lo, wr_hi, preferred_element_type=F32)
              + jnp.dot(hi, wr_lo, preferred_element_type=F32)) + br_ref[...]

    n_exp = logits.shape[1]
    eid = lax.broadcasted_iota(jnp.int32, logits.shape, 1).astype(F32)
    neg = jnp.full_like(logits, -jnp.inf)
    g = logits
    vals, idxs = [], []
    for _ in range(TOP_K):
        mx = jnp.max(g, axis=-1, keepdims=True)
        first = jnp.min(jnp.where(g == mx, eid, float(n_exp)), axis=-1, keepdims=True)
        vals.append(mx)
        idxs.append(first)
        g = jnp.where(eid == first, neg, g)
    exps = [jnp.exp(v - vals[0]) for v in vals]
    denom = exps[0] + exps[1] + exps[2] + exps[3]
    ids_ref[...] = _lanes_from_columns(idxs, jnp.int32)
    gates_ref[...] = _lanes_from_columns([e / denom for e in exps], F32)


def _router(h, norm_w, w_router, b_router, tm):
    t, d = h.shape
    n_exp = w_router.shape[1]
    return pl.pallas_call(
        _router_kernel,
        out_shape=(jax.ShapeDtypeStruct((t, d // 2), jnp.uint32),
                   jax.ShapeDtypeStruct((t, TOP_K), jnp.int32),
                   jax.ShapeDtypeStruct((t, TOP_K), F32)),
        grid=(t // tm,),
        in_specs=[pl.BlockSpec((tm, d), lambda i: (i, 0)),
                  pl.BlockSpec((1, d), lambda i: (0, 0)),
                  pl.BlockSpec((d, n_exp), lambda i: (0, 0)),
                  pl.BlockSpec((1, n_exp), lambda i: (0, 0))],
        out_specs=(pl.BlockSpec((tm, d // 2), lambda i: (i, 0)),
                   pl.BlockSpec((tm, TOP_K), lambda i: (i, 0)),
                   pl.BlockSpec((tm, TOP_K), lambda i: (i, 0))),
        compiler_params=_params("parallel"),
        name="rmsnorm2_router",
    )(h, norm_w.reshape(1, d), w_router, b_router.reshape(1, n_exp))


def _rank_kernel(ids_ref, rank_ref, counts_ref, carry_ref, *, n_exp):
    i = pl.program_id(0)

    @pl.when(i == 0)
    def _():
        carry_ref[...] = jnp.zeros_like(carry_ref)

    ids = ids_ref[...]
    tm = ids.shape[0]
    eid = lax.broadcasted_iota(jnp.int32, (tm, n_exp), 1)
    onehots = [(eid == ids[:, r:r + 1]).astype(F32) for r in range(TOP_K)]
    total = onehots[0] + onehots[1] + onehots[2] + onehots[3]
    row = lax.broadcasted_iota(jnp.int32, (tm, tm), 0)
    col = lax.broadcasted_iota(jnp.int32, (tm, tm), 1)
    strict_lower = (col < row).astype(BF16)
    before = jnp.dot(strict_lower, total.astype(BF16), preferred_element_type=F32)
    before = before + carry_ref[...]
    ranks = [jnp.sum(oh * before, axis=-1, keepdims=True) for oh in onehots]
    rank_ref[...] = _lanes_from_columns(ranks, jnp.int32)
    carry_ref[...] += jnp.sum(total, axis=0, keepdims=True)
    counts_ref[...] = carry_ref[...]


def _rank(ids, n_exp, tm):
    t = ids.shape[0]
    return pl.pallas_call(
        functools.partial(_rank_kernel, n_exp=n_exp),
        out_shape=(jax.ShapeDtypeStruct((t, TOP_K), jnp.int32),
                   jax.ShapeDtypeStruct((1, n_exp), F32)),
        grid=(t // tm,),
        in_specs=[pl.BlockSpec((tm, TOP_K), lambda i: (i, 0))],
        out_specs=(pl.BlockSpec((tm, TOP_K), lambda i: (i, 0)),
                   pl.BlockSpec((1, n_exp), lambda i: (0, 0))),
        scratch_shapes=[pltpu.VMEM((1, n_exp), F32)],
        compiler_params=_params("arbitrary"),
        name="expert_rank",
    )(ids)


def _dispatch_kernel(fill_start_ref, fill_n_ref, pos_ref, x_ref, buf_ref,
                     zeros_ref, sem, zsem, *, n_exp, tile_rows):
    i = pl.program_id(0)
    tm = x_ref.shape[0]
    zrows = zeros_ref.shape[0]
    sub = SUBLANES
    n_bits = (zrows // sub - 1).bit_length()

    def head_copy(e, r):
        a = fill_start_ref[e]
        n_head = jnp.minimum((-a) & (sub - 1), fill_n_ref[e])
        return r < n_head, pltpu.make_async_copy(
            zeros_ref.at[pl.ds(0, 1), :], buf_ref.at[pl.ds(a + r, 1), :], zsem)

    def body_copy(e, bit):
        a = fill_start_ref[e]
        n_head = jnp.minimum((-a) & (sub - 1), fill_n_ref[e])
        groups = (fill_n_ref[e] - n_head) // sub
        size = sub << bit
        dst = pl.multiple_of(a + n_head + sub * (groups & ((1 << bit) - 1)), sub)
        return (groups & (1 << bit)) != 0, pltpu.make_async_copy(
            zeros_ref.at[pl.ds(0, size), :], buf_ref.at[pl.ds(dst, size), :], zsem)

    def fill_copies():
        for e in range(n_exp):
            for r in range(sub - 1):
                yield head_copy(e, r)
            for bit in range(n_bits):
                yield body_copy(e, bit)

    def tail_copy(c):
        dst = pl.multiple_of(c * zrows, zrows)
        return pltpu.make_async_copy(zeros_ref, buf_ref.at[pl.ds(dst, zrows), :], zsem)

    @pl.when(i == 0)
    def _():
        zeros_ref[...] = jnp.zeros(zeros_ref.shape, zeros_ref.dtype)
        first_tail = fill_start_ref[n_exp] // zrows
        n_chunks = buf_ref.shape[0] // zrows

        def tail_start(c, carry):
            tail_copy(c).start()
            return carry

        def tail_wait(c, carry):
            tail_copy(c).wait()
            return carry

        for live, cp in fill_copies():
            pl.when(live)(cp.start)
        lax.fori_loop(first_tail, n_chunks, tail_start, 0)
        for live, cp in fill_copies():
            pl.when(live)(cp.wait)
        lax.fori_loop(first_tail, n_chunks, tail_wait, 0)

    def start(tok, carry):
        src = x_ref.at[pl.ds(tok, 1), :]
        for slot in range(TOP_K):
            dst = buf_ref.at[pl.ds(pos_ref[tok * TOP_K + slot], 1), :]
            pltpu.make_async_copy(src, dst, sem).start()
        return carry

    lax.fori_loop(0, tm, start, 0, unroll=DMA_ISSUE_UNROLL)
    for _ in range(TOP_K):
        pltpu.make_async_copy(x_ref, buf_ref.at[pl.ds(0, tm), :], sem).wait()


def _dispatch(fill_start, fill_n, pos_flat, xp, n_rows, tm, tile_rows):
    t, w = xp.shape
    n_exp = fill_n.shape[0]
    zrows = tile_rows
    grid_spec = pltpu.PrefetchScalarGridSpec(
        num_scalar_prefetch=2,
        grid=(t // tm,),
        in_specs=[pl.BlockSpec((tm * TOP_K,), lambda i, fs, fn: (i,), memory_space=pltpu.SMEM),
                  pl.BlockSpec((tm, w), lambda i, fs, fn: (i, 0))],
        out_specs=pl.BlockSpec(memory_space=pl.ANY),
        scratch_shapes=[pltpu.VMEM((zrows, w), xp.dtype),
                        pltpu.SemaphoreType.DMA(()),
                        pltpu.SemaphoreType.DMA(())],
    )
    return pl.pallas_call(
        functools.partial(_dispatch_kernel, n_exp=n_exp, tile_rows=tile_rows),
        out_shape=jax.ShapeDtypeStruct((n_rows, w), xp.dtype),
        grid_spec=grid_spec,
        compiler_params=_params("arbitrary"),
        name="dispatch",
    )(fill_start, fill_n, pos_flat, xp)


def _deinterleave_matrix(n):
    r = lax.broadcasted_iota(jnp.int32, (n, n), 0)
    c = lax.broadcasted_iota(jnp.int32, (n, n), 1)
    src = jnp.where(c < n // 2, 2 * c, 2 * (c - n // 2) + 1)
    return (r == src).astype(BF16)


TILE_EMPTY, TILE_HALF, TILE_FULL = 0, 1, 2


def _for_tile_fill(fill, o_ref, compute):
    tm = o_ref.shape[0]
    pl.when(fill == TILE_FULL)(functools.partial(compute, tm))
    pl.when(fill == TILE_HALF)(functools.partial(compute, tm // 2))

    @pl.when(fill == TILE_EMPTY)
    def _():
        o_ref[...] = jnp.zeros_like(o_ref)


def _stage_expert_weights(w_hbm, stage_ref, sem, tabs, convert):
    te_ref, tf_ref, ne_ref, nw_ref = tabs
    c = pl.program_id(0)
    i = pl.program_id(1)
    width = stage_ref.shape[1]

    def block_copy(e, chunk):
        col = pl.multiple_of(chunk * width, width)
        return pltpu.make_async_copy(w_hbm.at[e, :, pl.ds(col, width)], stage_ref, sem)

    @pl.when((c == 0) & (i == 0))
    def _():
        block_copy(te_ref[0], 0).start()

    @pl.when(tf_ref[i] == 1)
    def _():
        block_copy(te_ref[i], c).wait()
        convert()
        next_chunk = c + nw_ref[i]

        @pl.when(next_chunk < pl.num_programs(0))
        def _():
            block_copy(ne_ref[i], next_chunk).start()


def _gateup_kernel(te_ref, tv_ref, tf_ref, tr_ref, ne_ref, nw_ref, x_ref, w_hbm, bg_ref, bu_ref,
                   o_ref, stage_ref, wg_ref, wu_ref, sem):
    del tr_ref
    i = pl.program_id(1)
    grp = V7X_MXU_DIM
    half = grp // 2

    def convert():
        perm = _deinterleave_matrix(grp)
        for g in range(stage_ref.shape[1] // grp):
            w = stage_ref[:, g * grp:(g + 1) * grp].astype(BF16)
            wp = jnp.dot(w, perm, preferred_element_type=F32).astype(BF16)
            wg_ref[:, g * half:(g + 1) * half] = wp[:, :half]
            wu_ref[:, g * half:(g + 1) * half] = wp[:, half:]

    _stage_expert_weights(w_hbm, stage_ref, sem, (te_ref, tf_ref, ne_ref, nw_ref), convert)

    def compute(rows):
        xa, xb = _unpack_cols(x_ref[:rows, :])
        k_half = xa.shape[1]

        def proj(w_s, b_ref):
            return (jnp.dot(xa, w_s[:k_half, :], preferred_element_type=F32)
                    + jnp.dot(xb, w_s[k_half:, :], preferred_element_type=F32)
                    + b_ref[...])

        g = jnp.minimum(proj(wg_ref, bg_ref), SWIGLU_LIMIT)
        u = jnp.clip(proj(wu_ref, bu_ref), -SWIGLU_LIMIT, SWIGLU_LIMIT)
        sig = 1.0 / (1.0 + jnp.exp(-(g * SWIGLU_ALPHA)))
        o_ref[:rows, :] = ((u + 1.0) * (g * sig)).astype(o_ref.dtype)
        if rows < o_ref.shape[0]:
            o_ref[rows:, :] = jnp.zeros((o_ref.shape[0] - rows, o_ref.shape[1]), o_ref.dtype)

    _for_tile_fill(tv_ref[i], o_ref, compute)


def _gateup(tabs, xs, w_gu, bg, bu, tm, fc):
    rows, k_half = xs.shape
    n_exp, d, f2 = w_gu.shape
    f = f2 // 2
    n_tiles = rows // tm
    n_chunks = f // fc

    def b_map(c, i, te, tv, tf, tr, ne, nw):
        return (te[i], 0, c)

    grid_spec = pltpu.PrefetchScalarGridSpec(
        num_scalar_prefetch=6,
        grid=(n_chunks, n_tiles),
        in_specs=[pl.BlockSpec((tm, k_half), lambda c, i, te, tv, tf, tr, ne, nw: (tr[i], 0)),
                  pl.BlockSpec(memory_space=pl.ANY),
                  pl.BlockSpec((None, 1, fc), b_map),
                  pl.BlockSpec((None, 1, fc), b_map)],
        out_specs=pl.BlockSpec((tm, fc), lambda c, i, te, tv, tf, tr, ne, nw: (i, c)),
        scratch_shapes=[pltpu.VMEM((d, 2 * fc), F32),
                        pltpu.VMEM((d, fc), BF16), pltpu.VMEM((d, fc), BF16),
                        pltpu.SemaphoreType.DMA(())],
    )
    return pl.pallas_call(
        _gateup_kernel,
        out_shape=jax.ShapeDtypeStruct((rows, f), BF16),
        grid_spec=grid_spec,
        compiler_params=_params("arbitrary", "arbitrary"),
        name="expert_gate_up",
    )(*tabs, xs, w_gu, bg, bu)


def _down_kernel(te_ref, tv_ref, tf_ref, ne_ref, nw_ref, a_ref, w_hbm, b_ref, o_ref,
                 stage_ref, wb_ref, sem):
    i = pl.program_id(1)

    def convert():
        wb_ref[...] = stage_ref[...].astype(BF16)

    _stage_expert_weights(w_hbm, stage_ref, sem, (te_ref, tf_ref, ne_ref, nw_ref), convert)

    def compute(rows):
        o = jnp.dot(a_ref[:rows, :], wb_ref[...], preferred_element_type=F32) + b_ref[...]
        ob = o.astype(BF16).astype(F32)
        half = ob.shape[1] // 2
        o_ref[:rows, :] = _pack_cols(ob[:, :half], ob[:, half:])
        if rows < o_ref.shape[0]:
            o_ref[rows:, :] = jnp.zeros((o_ref.shape[0] - rows, o_ref.shape[1]), o_ref.dtype)

    _for_tile_fill(tv_ref[i], o_ref, compute)


def _down(tabs, act, wd, bd, tm, tn):
    rows, f = act.shape
    n_exp, _, d = wd.shape
    n_chunks = d // tn

    grid_spec = pltpu.PrefetchScalarGridSpec(
        num_scalar_prefetch=5,
        grid=(n_chunks, rows // tm),
        in_specs=[pl.BlockSpec((tm, f), lambda c, i, te, tv, tf, ne, nw: (i, 0)),
                  pl.BlockSpec(memory_space=pl.ANY),
                  pl.BlockSpec((None, 1, tn), lambda c, i, te, tv, tf, ne, nw: (te[i], 0, c))],
        out_specs=pl.BlockSpec((tm, tn // 2), lambda c, i, te, tv, tf, ne, nw: (i, c)),
        scratch_shapes=[pltpu.VMEM((f, tn), F32), pltpu.VMEM((f, tn), BF16),
                        pltpu.SemaphoreType.DMA(())],
    )
    return pl.pallas_call(
        _down_kernel,
        out_shape=jax.ShapeDtypeStruct((rows, d // 2), jnp.uint32),
        grid_spec=grid_spec,
        compiler_params=_params("arbitrary", "arbitrary"),
        name="expert_down",
    )(*tabs, act, wd, bd)


def _combine_kernel(pos_ref, pos_next_ref, gates_ref, h_ref, w_ref, y_ref, o_ref, rows_ref, sems,
                    *, packed_chunk):
    i = pl.program_id(0)
    tm = h_ref.shape[0]
    cur = i & 1

    def request(p_ref, buf):
        def start(tok, carry):
            for slot in range(TOP_K):
                src = y_ref.at[pl.ds(p_ref[tok * TOP_K + slot], 1), :]
                dst = rows_ref.at[buf, slot, pl.ds(tok, 1), :]
                pltpu.make_async_copy(src, dst, sems.at[buf]).start()
            return carry

        lax.fori_loop(0, tm, start, 0, unroll=DMA_ISSUE_UNROLL)

    pl.when(i == 0)(functools.partial(request, pos_ref, 0))
    pl.when(i + 1 < pl.num_programs(0))(functools.partial(request, pos_next_ref, 1 - cur))
    for r in range(TOP_K):
        pltpu.make_async_copy(y_ref.at[pl.ds(0, tm), :], rows_ref.at[cur, r], sems.at[cur]).wait()

    gates = gates_ref[...]
    lo, hi = _unpack_cols_f32(rows_ref[cur, 0])
    moe_lo = gates[:, 0:1] * lo
    moe_hi = gates[:, 0:1] * hi
    for r in range(1, TOP_K):
        lo, hi = _unpack_cols_f32(rows_ref[cur, r])
        moe_lo += gates[:, r:r + 1] * lo
        moe_hi += gates[:, r:r + 1] * hi
    pieces = []
    for c in range(moe_lo.shape[1] // packed_chunk):
        cols = slice(c * packed_chunk, (c + 1) * packed_chunk)
        pieces += [moe_lo[:, cols], moe_hi[:, cols]]
    x = h_ref[...] + jnp.concatenate(pieces, axis=1)
    ms = jnp.mean(x * x, axis=-1, keepdims=True)
    o_ref[...] = (x * lax.rsqrt(ms + NORM_EPS)) * w_ref[...]


def _combine(pos_flat, gates, h, norm_w, ys, tm, down_cols):
    t, d = h.shape
    last = t // tm - 1
    return pl.pallas_call(
        functools.partial(_combine_kernel, packed_chunk=down_cols // 2),
        out_shape=jax.ShapeDtypeStruct((t, d), F32),
        grid=(t // tm,),
        in_specs=[pl.BlockSpec((tm * TOP_K,), lambda i: (i,), memory_space=pltpu.SMEM),
                  pl.BlockSpec((tm * TOP_K,), lambda i: (jnp.minimum(i + 1, last),),
                               memory_space=pltpu.SMEM),
                  pl.BlockSpec((tm, TOP_K), lambda i: (i, 0)),
                  pl.BlockSpec((tm, d), lambda i: (i, 0)),
                  pl.BlockSpec((1, d), lambda i: (0, 0)),
                  pl.BlockSpec(memory_space=pl.ANY)],
        out_specs=pl.BlockSpec((tm, d), lambda i: (i, 0)),
        scratch_shapes=[pltpu.VMEM((2, TOP_K, tm, d // 2), jnp.uint32),
                        pltpu.SemaphoreType.DMA((2,))],
        compiler_params=_params("arbitrary"),
        name="combine_final_norm",
    )(pos_flat, pos_flat, gates, h, norm_w.reshape(1, d), ys)


def _tiles(t, s, d, f):
    return dict(
        norm_rows=min(256, t),
        mm_rows=min(1024, t),
        mm_cols=min(1024, d),
        attn_tile=min(512, s),
        router_rows=min(256, t),
        rank_rows=min(512, t),
        dispatch_rows=min(256, t),
        expert_rows=min(512, t),
        ff_chunk=min(512, f),
        down_cols=min(2048, d),
        combine_rows=min(256, t),
    )


def _alibi_slopes(n):
    return jnp.exp2(-ALIBI_MAX_BIAS * jnp.arange(1, n + 1, dtype=F32) / n)


def _routing_tables(ids, rank, counts, tm, n_tiles):
    n_exp = counts.shape[0]
    i32 = jnp.int32
    padded = ((counts + tm - 1) // tm) * tm
    ends = jnp.cumsum(padded)
    starts = ends - padded
    onehot = ids[..., None] == jnp.arange(n_exp, dtype=i32)
    pos = rank + jnp.sum(jnp.where(onehot, starts, 0), axis=-1)
    tile_start = jnp.arange(n_tiles, dtype=i32) * tm
    tile_valid = (tile_start < ends[-1]).astype(i32)
    tile_expert = jnp.minimum(jnp.sum((tile_start[:, None] >= ends[None, :]).astype(i32), axis=1),
                              n_exp - 1)
    prev_expert = jnp.concatenate([jnp.full((1,), -1, i32), tile_expert[:-1]])
    tile_first = tile_valid * (tile_expert != prev_expert).astype(i32)
    of_expert = tile_expert[:, None] == jnp.arange(n_exp, dtype=i32)[None, :]
    rows_end = jnp.sum(jnp.where(of_expert, (starts + counts)[None, :], 0), axis=1)
    tile_rows = jnp.clip(rows_end - tile_start, 0, tm)
    tile_fill = tile_valid * jnp.where(tile_rows <= tm // 2, TILE_HALF, TILE_FULL).astype(i32)
    tile_idx = jnp.arange(n_tiles, dtype=i32)
    later_first = (tile_idx[None, :] > tile_idx[:, None]) & (tile_first[None, :] == 1)
    next_first = jnp.min(jnp.where(later_first, tile_idx[None, :], n_tiles), axis=1)
    next_wraps = (next_first == n_tiles).astype(i32)
    next_expert = jnp.sum(jnp.where(tile_idx[None, :] == (next_first % n_tiles)[:, None],
                                    tile_expert[None, :], 0), axis=1).astype(i32)
    tile_row = jnp.minimum(jnp.arange(n_tiles, dtype=i32), ends[-1] // tm - 1)
    fill_start = jnp.concatenate([starts + counts, ends[-1:]]).astype(i32)
    fill_n = (padded - counts).astype(i32)
    tiles = dict(expert=tile_expert, fill=tile_fill, first=tile_first, row=tile_row,
                 next_expert=next_expert, next_wraps=next_wraps)
    return pos.reshape(-1), tiles, fill_start, fill_n


def _layer(h, l, norm1_w, w_in, lam_q1, lam_k1, lam_q2, lam_k2, subln_w, w_out, norm2_w,
           w_router, b_router, w_gate_up, b_gate_up, w_down, b_down, out_norm_w):
    b, s, d = h.shape
    t = b * s
    n_exp = w_router.shape[-1]
    f = w_down.shape[-2]
    n_heads = d // 2 // HEAD_DIM
    width = n_heads * HEAD_DIM
    tl = _tiles(t, s, d, f)
    lambda_init = 0.8 - 0.6 * math.exp(-0.3 * l)
    x2 = h.reshape(t, d)

    n1 = _rmsnorm(x2, norm1_w[l], tl["norm_rows"])
    n_cols = w_in.shape[-1]
    col_scale = jnp.ones((n_cols,), F32)
    col_scale = col_scale.at[:width].set(LOG2_E * DIFF_QK_DIM ** -0.5)
    col_scale = col_scale.at[3 * width:4 * width].set(LOG2_E * HEAD_DIM ** -0.5)
    proj = _matmul(n1, w_in[l].astype(BF16), col_scale.reshape(1, n_cols), tl["mm_rows"],
                   tl["mm_cols"], BF16)
    proj3 = proj.reshape(b, s, proj.shape[1])
    slopes = _alibi_slopes(n_heads)
    lam4 = jnp.stack([lam_q1[l], lam_k1[l], lam_q2[l], lam_k2[l]]).astype(F32)
    blocks = width // HEAD_DIM
    y_a = _diff_attention(proj3, slopes, lam4, subln_w[l].reshape(1, HEAD_DIM).astype(F32),
                          n_heads, 0, blocks, 2 * blocks, tl["attn_tile"], lambda_init)
    y_b = _moba_attention(proj3, slopes, n_heads, 3 * blocks, 4 * blocks, 5 * blocks,
                          tl["attn_tile"])
    h1 = _outproj(y_a.reshape(t, width), y_b.reshape(t, width), w_out[l].astype(BF16), x2,
                  tl["mm_rows"], tl["mm_cols"])

    xp, ids, gates = _router(h1, norm2_w[l], w_router[l], b_router[l], tl["router_rows"])
    rank, counts = _rank(ids, n_exp, tl["rank_rows"])
    tm = tl["expert_rows"]
    n_tiles = (t * TOP_K) // tm + n_exp
    pos_flat, tiles, fill_start, fill_n = _routing_tables(
        ids, rank, counts.reshape(n_exp).astype(jnp.int32), tm, n_tiles)

    xs = _dispatch(fill_start, fill_n, pos_flat, xp, n_tiles * tm, tl["dispatch_rows"], tm)
    bgu = b_gate_up[l].reshape(n_exp, 1, f, 2)
    gateup_tabs = (tiles["expert"], tiles["fill"], tiles["first"], tiles["row"],
                   tiles["next_expert"], tiles["next_wraps"])
    act = _gateup(gateup_tabs, xs, w_gate_up[l], bgu[..., 0], bgu[..., 1], tm, tl["ff_chunk"])
    down_tabs = (tiles["expert"], tiles["fill"], tiles["first"],
                 tiles["next_expert"], tiles["next_wraps"])
    ys = _down(down_tabs, act, w_down[l], b_down[l].reshape(n_exp, 1, d), tm, tl["down_cols"])
    return _combine(pos_flat, gates, h1, out_norm_w, ys, tl["combine_rows"],
                    tl["down_cols"]).reshape(b, s, d)


def kernel(x, norm1_w, w_in, lam_q1, lam_k1, lam_q2, lam_k2, subln_w, w_out, norm2_w,
           w_router, b_router, w_gate_up, b_gate_up, w_down, b_down, final_norm_w):
    depth = w_in.shape[0]
    assert depth == 1, "the fused combine + final norm stage assumes a single layer"
    return _layer(x, 0, norm1_w, w_in, lam_q1, lam_k1, lam_q2, lam_k2, subln_w, w_out,
                  norm2_w, w_router, b_router, w_gate_up, b_gate_up, w_down, b_down,
                  final_norm_w)
```

```python
import functools
import math

import jax
import jax.numpy as jnp
from jax import lax
from jax.experimental import pallas as pl
from jax.experimental.pallas import tpu as pltpu

F32 = jnp.float32
BF16 = jnp.bfloat16

HEAD_DIM = 128
DIFF_QK_DIM = HEAD_DIM // 2
MOBA_BLOCK = 256
MOBA_TOPK = 3
TOP_K = 4
SWIGLU_LIMIT = 7.0
SWIGLU_ALPHA = 1.702
NORM_EPS = 1e-5
ALIBI_MAX_BIAS = 8.0

V7X_VMEM_BYTES = 64 * 1024 * 1024
VMEM_LIMIT_BYTES = V7X_VMEM_BYTES - 8 * 1024 * 1024
V7X_MXU_DIM = 256
SUBLANES = 8
LANES = 128
DMA_ISSUE_UNROLL = 8


def _params(*semantics):
    return pltpu.CompilerParams(dimension_semantics=semantics,
                                vmem_limit_bytes=VMEM_LIMIT_BYTES)


def _rmsnorm_kernel(x_ref, w_ref, o_ref):
    x = x_ref[...]
    ms = jnp.mean(x * x, axis=-1, keepdims=True)
    o_ref[...] = ((x * lax.rsqrt(ms + NORM_EPS)) * w_ref[...]).astype(o_ref.dtype)


def _rmsnorm(x, w, tm):
    t, d = x.shape
    return pl.pallas_call(
        _rmsnorm_kernel,
        out_shape=jax.ShapeDtypeStruct((t, d), BF16),
        grid=(t // tm,),
        in_specs=[pl.BlockSpec((tm, d), lambda i: (i, 0)),
                  pl.BlockSpec((1, d), lambda i: (0, 0))],
        out_specs=pl.BlockSpec((tm, d), lambda i: (i, 0)),
        compiler_params=_params("parallel"),
        name="rmsnorm1",
    )(x, w.reshape(1, d))


def _matmul_kernel(a_ref, b_ref, cs_ref, o_ref):
    acc = jnp.dot(a_ref[...], b_ref[...], preferred_element_type=F32)
    o_ref[...] = (acc * cs_ref[...]).astype(o_ref.dtype)


def _matmul(a, b, col_scale, tm, tn, out_dtype):
    m, k = a.shape
    _, n = b.shape
    return pl.pallas_call(
        _matmul_kernel,
        out_shape=jax.ShapeDtypeStruct((m, n), out_dtype),
        grid=(m // tm, n // tn),
        in_specs=[pl.BlockSpec((tm, k), lambda i, j: (i, 0)),
                  pl.BlockSpec((k, tn), lambda i, j: (0, j)),
                  pl.BlockSpec((1, tn), lambda i, j: (0, j))],
        out_specs=pl.BlockSpec((tm, tn), lambda i, j: (i, j)),
        compiler_params=_params("parallel", "parallel"),
        name="in_proj",
    )(a, b, col_scale)


def _outproj_kernel(ya_ref, yb_ref, w_ref, x_ref, o_ref):
    ka = ya_ref.shape[1]
    acc = jnp.dot(ya_ref[...], w_ref[:ka, :], preferred_element_type=F32)
    acc += jnp.dot(yb_ref[...], w_ref[ka:, :], preferred_element_type=F32)
    o_ref[...] = x_ref[...] + acc


def _outproj(ya, yb, w, x, tm, tn):
    m, ka = ya.shape
    kb = yb.shape[1]
    n = w.shape[1]
    return pl.pallas_call(
        _outproj_kernel,
        out_shape=jax.ShapeDtypeStruct((m, n), F32),
        grid=(m // tm, n // tn),
        in_specs=[pl.BlockSpec((tm, ka), lambda i, j: (i, 0)),
                  pl.BlockSpec((tm, kb), lambda i, j: (i, 0)),
                  pl.BlockSpec((ka + kb, tn), lambda i, j: (0, j)),
                  pl.BlockSpec((tm, tn), lambda i, j: (i, j))],
        out_specs=pl.BlockSpec((tm, tn), lambda i, j: (i, j)),
        compiler_params=_params("parallel", "parallel"),
        name="out_proj",
    )(ya, yb, w, x)


RUNNING_MAX_INIT = -1e30
LOG2_E = 1.4426950408889634


def _qk(q, k):
    return lax.dot_general(q, k, (((1,), (1,)), ((), ())), preferred_element_type=F32)


def _pair_tables(n_tiles):
    qi = [i for i in range(n_tiles) for _ in range(i + 1)]
    kj = [j for i in range(n_tiles) for j in range(i + 1)]
    return jnp.asarray(qi, jnp.int32), jnp.asarray(kj, jnp.int32)


def _fill_vext(vext_ref, v_ref):
    vext_ref[:, :HEAD_DIM] = v_ref[...]
    vext_ref[:, HEAD_DIM:] = jnp.ones((v_ref.shape[0], HEAD_DIM), vext_ref.dtype)


def _fill_tile_bias(bias_ref, col_bias_scale, t):
    row = lax.broadcasted_iota(jnp.int32, (t, t), 0)
    col = lax.broadcasted_iota(jnp.int32, (t, t), 1)
    cb = col_bias_scale * col.astype(F32)
    bias_ref[0] = cb
    bias_ref[1] = jnp.where(col <= row, cb, -jnp.inf)


def _widen(stat, n):
    return stat if stat.shape[1] == 1 else jnp.tile(stat, (1, n // stat.shape[1]))


def _attn_scratch(s_len, t, n_comp):
    stat = LANES if n_comp == 1 else 1
    return [pltpu.VMEM((s_len, 2 * HEAD_DIM), BF16),
            pltpu.VMEM((2, t, t), F32),
            pltpu.VMEM((n_comp, t, t), F32),
            pltpu.VMEM((n_comp, t, t), F32),
            pltpu.VMEM((n_comp, t, t), BF16),
            pltpu.VMEM((n_comp, t, t), BF16),
            pltpu.VMEM((n_comp, t, stat), F32),
            pltpu.VMEM((n_comp, t, stat), F32),
            pltpu.VMEM((n_comp, t, stat), F32),
            pltpu.VMEM((n_comp, t, 2 * HEAD_DIM), F32),
            pltpu.VMEM((s_len // t + 1, n_comp, t, 2 * HEAD_DIM), F32)]


def _attn_pipeline(qi_ref, kj_ref, n_pairs, t, n_comp, scores, shift_step, rows_can_be_empty,
                   vext_ref, bufs, finalize):
    s_bufs, p_bufs, a_bufs, m_ref, acc_ref, done_ref = bufs
    for buf in s_bufs + p_bufs:
        buf[...] = jnp.zeros(buf.shape, buf.dtype)
    for buf in a_bufs:
        buf[...] = jnp.ones(buf.shape, buf.dtype)
    m_ref[...] = jnp.full(m_ref.shape, RUNNING_MAX_INIT, F32)
    acc_ref[...] = jnp.zeros(acc_ref.shape, F32)

    def pair(n):
        nc = jnp.clip(n, 0, n_pairs - 1)
        return qi_ref[nc], kj_ref[nc], (n >= 0) & (n < n_pairs)

    def stage_scores(n, s_buf):
        i, j, _ = pair(n)
        for c in range(n_comp):
            s_buf[c] = scores(i, j, c)

    def stage_softmax(n, s_buf, p_buf, a_buf):
        _, j, _ = pair(n)
        first = j == 0
        max_cap = jnp.where(first, RUNNING_MAX_INIT, jnp.inf)
        keep = jnp.where(first, 0.0, 1.0)
        for c in range(n_comp):
            m_prev = jnp.minimum(m_ref[c], max_cap)
            row_max = jnp.broadcast_to(jnp.max(s_buf[c], axis=-1, keepdims=True), m_prev.shape)
            m_cur = jnp.maximum(m_prev, row_max)
            m_ref[c] = m_cur - shift_step
            p_buf[c] = jnp.exp2(s_buf[c] - _widen(m_cur, t)).astype(p_buf.dtype)
            rescale = jnp.exp2(m_prev - m_cur)
            a_buf[c] = rescale * keep if rows_can_be_empty else rescale

    n_tiles = done_ref.shape[0] - 1

    def stage_values(n, p_buf, a_buf):
        i, j, active = pair(n)
        vext = vext_ref[pl.ds(pl.multiple_of(j * t, t), t), :]
        slot = jnp.where(active, i, n_tiles)
        for c in range(n_comp):
            acc = (_widen(a_buf[c], 2 * HEAD_DIM) * acc_ref[c]
                   + jnp.dot(p_buf[c], vext, preferred_element_type=F32))
            acc_ref[c] = acc
            done_ref[slot, c] = acc

    def body(step, carry):
        for parity in range(2):
            n = 2 * step + parity
            stage_values(n - 2, p_bufs[parity], a_bufs[parity])
            stage_scores(n, s_bufs[parity])
            stage_softmax(n - 1, s_bufs[1 - parity], p_bufs[1 - parity], a_bufs[1 - parity])
        return carry

    lax.fori_loop(0, (n_pairs + 3) // 2, body, 0)
    for i in range(n_tiles):
        finalize(i, done_ref[i])


def _split_bufs(scratch):
    s0, s1, p0, p1, a0, a1, m_ref, acc_ref, done_ref = scratch
    return (s0, s1), (p0, p1), (a0, a1), m_ref, acc_ref, done_ref


def _diff_attn_kernel(slopes_ref, qi_ref, kj_ref, lam_ref, subw_ref, q_ref, k_ref, v_ref, o_ref,
                      qc_ref, vext_ref, bias_ref, *scratch, t, n_pairs, lambda_init):
    slope = slopes_ref[pl.program_id(1)]
    dq = DIFF_QK_DIM
    _fill_vext(vext_ref, v_ref)
    _fill_tile_bias(bias_ref, LOG2_E * slope, t)

    q_all = q_ref[...]
    lane = lax.broadcasted_iota(jnp.int32, q_all.shape, 1)
    qc_ref[0] = jnp.where(lane < dq, q_all, jnp.zeros_like(q_all))
    qc_ref[1] = jnp.where(lane >= dq, q_all, jnp.zeros_like(q_all))

    def scores(i, j, c):
        q = qc_ref[c, pl.ds(pl.multiple_of(i * t, t), t), :]
        k = k_ref[pl.ds(pl.multiple_of(j * t, t), t), :]
        return _qk(q, k) + bias_ref[(i == j).astype(jnp.int32)]

    bufs = _split_bufs(scratch)

    def finalize(i, acc):
        lam_v = lam_ref[...]
        lam = (jnp.exp(jnp.sum(lam_v[0:1, :] * lam_v[1:2, :], axis=-1, keepdims=True))
               - jnp.exp(jnp.sum(lam_v[2:3, :] * lam_v[3:4, :], axis=-1, keepdims=True))
               + lambda_init)
        a1 = acc[0]
        a2 = acc[1]
        o = (a1[:, :HEAD_DIM] / a1[:, HEAD_DIM:HEAD_DIM + 1]
             - lam * (a2[:, :HEAD_DIM] / a2[:, HEAD_DIM:HEAD_DIM + 1]))
        ms = jnp.mean(o * o, axis=-1, keepdims=True)
        o = (o * lax.rsqrt(ms + NORM_EPS)) * subw_ref[...]
        o_ref[i * t:(i + 1) * t, :] = (o * (1.0 - lambda_init)).astype(o_ref.dtype)

    _attn_pipeline(qi_ref, kj_ref, n_pairs, t, 2, scores, LOG2_E * slope * t, False,
                   vext_ref, bufs, finalize)


def _head_spec(s, blk):
    return pl.BlockSpec((None, s, HEAD_DIM), lambda bi, h: (bi, 0, blk + h))


def _diff_attention(proj3, slopes, lam4, subw, n_heads, q_blk, k_blk, v_blk, t, lambda_init):
    b, s, _ = proj3.shape
    qi, kj = _pair_tables(s // t)
    kern = functools.partial(_diff_attn_kernel, t=t, n_pairs=qi.shape[0], lambda_init=lambda_init)
    smem = pl.BlockSpec(memory_space=pltpu.SMEM)
    return pl.pallas_call(
        kern,
        out_shape=jax.ShapeDtypeStruct((b, s, n_heads * HEAD_DIM), BF16),
        grid=(b, n_heads),
        in_specs=[smem, smem, smem,
                  pl.BlockSpec((4, DIFF_QK_DIM), lambda bi, h: (0, 0)),
                  pl.BlockSpec((1, HEAD_DIM), lambda bi, h: (0, 0)),
                  _head_spec(s, q_blk), _head_spec(s, k_blk), _head_spec(s, v_blk)],
        out_specs=_head_spec(s, 0),
        scratch_shapes=[pltpu.VMEM((2, s, HEAD_DIM), BF16)] + _attn_scratch(s, t, 2),
        compiler_params=_params("parallel", "parallel"),
        name="diff_attention",
    )(slopes, qi, kj, lam4, subw, proj3, proj3, proj3)


MASKED_SCORE = -(2.0 ** 127)


def _moba_kernel(slopes_ref, qi_ref, kj_ref, q_ref, k_ref, v_ref, o_ref,
                 kmean_ref, qaug_ref, kaug_ref, vext_ref, bias_ref, *scratch,
                 t, n_pairs, n_blocks):
    slope = slopes_ref[pl.program_id(1)]
    blk = MOBA_BLOCK
    blk_shift = blk.bit_length() - 1
    s_len = q_ref.shape[0]

    _fill_vext(vext_ref, v_ref)
    _fill_tile_bias(bias_ref, LOG2_E * slope, t)

    kmean_ref[...] = jnp.zeros(kmean_ref.shape, F32)
    for n in range(n_blocks):
        kb = k_ref[n * blk:(n + 1) * blk, :].astype(F32)
        kmean_ref[n:n + 1, :] = jnp.mean(kb, axis=0, keepdims=True)
    km = kmean_ref[...]
    km_hi = km.astype(BF16)
    km_lo = (km - km_hi.astype(F32)).astype(BF16)

    nb_pad = -(-n_blocks // SUBLANES) * SUBLANES
    blk_row = lax.broadcasted_iota(jnp.int32, (nb_pad, t), 0)
    blk_rowf = blk_row.astype(F32)
    q_pos = lax.broadcasted_iota(jnp.int32, (1, t), 1)
    lane = lax.broadcasted_iota(jnp.int32, (t, HEAD_DIM), 1)
    row = lax.broadcasted_iota(jnp.int32, (t, 1), 0)
    for it in range(s_len // t):
        rows = slice(it * t, (it + 1) * t)
        q = q_ref[rows, :]
        k = k_ref[rows, :]
        own = (it * t + q_pos) >> blk_shift
        gate = (_qk(km_hi, q) + _qk(km_lo, q))[:nb_pad, :]
        neg = jnp.full_like(gate, -jnp.inf)
        g = jnp.where(blk_row < own, gate, neg)
        visible = blk_row == own
        for _ in range(min(MOBA_TOPK, n_blocks)):
            mx = jnp.max(g, axis=0, keepdims=True)
            first = jnp.min(jnp.where(g == mx, blk_rowf, float(nb_pad)), axis=0, keepdims=True)
            pick = (blk_rowf == first) & (mx > -jnp.inf)
            visible = visible | pick
            g = jnp.where(pick, neg, g)
        bias_t = jnp.where(visible | (blk_row >= n_blocks), 0.0, MASKED_SCORE)
        bias_t = jnp.concatenate([bias_t, jnp.zeros((HEAD_DIM - nb_pad, t), F32)], axis=0)
        qaug_ref[rows, :] = jnp.concatenate([q, bias_t.T.astype(BF16)], axis=1)
        own_col = (it * t + row) >> blk_shift
        kaug_ref[rows, :] = jnp.concatenate([k, (lane == own_col).astype(BF16)], axis=1)

    def scores(i, j, c):
        q = qaug_ref[pl.ds(pl.multiple_of(i * t, t), t), :]
        k = kaug_ref[pl.ds(pl.multiple_of(j * t, t), t), :]
        return _qk(q, k) + bias_ref[(i == j).astype(jnp.int32)]

    bufs = _split_bufs(scratch)

    def finalize(i, acc):
        acc = acc[0]
        o_ref[i * t:(i + 1) * t, :] = (
            acc[:, :HEAD_DIM] / acc[:, HEAD_DIM:HEAD_DIM + 1]).astype(o_ref.dtype)

    _attn_pipeline(qi_ref, kj_ref, n_pairs, t, 1, scores, LOG2_E * slope * t, True,
                   vext_ref, bufs, finalize)


def _moba_attention(proj3, slopes, n_heads, q_blk, k_blk, v_blk, t):
    b, s, _ = proj3.shape
    n_blocks = s // MOBA_BLOCK
    assert n_blocks <= HEAD_DIM, "one lane per key block"
    qi, kj = _pair_tables(s // t)
    kern = functools.partial(_moba_kernel, t=t, n_pairs=qi.shape[0], n_blocks=n_blocks)
    smem = pl.BlockSpec(memory_space=pltpu.SMEM)
    return pl.pallas_call(
        kern,
        out_shape=jax.ShapeDtypeStruct((b, s, n_heads * HEAD_DIM), BF16),
        grid=(b, n_heads),
        in_specs=[smem, smem, smem,
                  _head_spec(s, q_blk), _head_spec(s, k_blk), _head_spec(s, v_blk)],
        out_specs=_head_spec(s, 0),
        scratch_shapes=[pltpu.VMEM((HEAD_DIM, HEAD_DIM), F32),
                        pltpu.VMEM((s, 2 * HEAD_DIM), BF16),
                        pltpu.VMEM((s, 2 * HEAD_DIM), BF16)] + _attn_scratch(s, t, 1),
        compiler_params=_params("parallel", "parallel"),
        name="moba_attention",
    )(slopes, qi, kj, proj3, proj3, proj3)


def _pack_cols(a, b):
    ua = lax.bitcast_convert_type(a, jnp.uint32)
    ub = lax.bitcast_convert_type(b, jnp.uint32)
    return (ua >> 16) | (ub & jnp.uint32(0xFFFF0000))


def _unpack_cols_f32(w):
    lo = lax.bitcast_convert_type(w << 16, F32)
    hi = lax.bitcast_convert_type(w & jnp.uint32(0xFFFF0000), F32)
    return lo, hi


def _unpack_cols(w):
    lo, hi = _unpack_cols_f32(w)
    return lo.astype(BF16), hi.astype(BF16)


def _lanes_from_columns(cols, dtype):
    tm = cols[0].shape[0]
    lane = lax.broadcasted_iota(jnp.int32, (tm, len(cols)), 1)
    out = jnp.zeros((tm, len(cols)), dtype)
    for r, c in enumerate(cols):
        out = jnp.where(lane == r, c.astype(dtype), out)
    return out


def _router_kernel(h_ref, w_ref, wr_ref, br_ref, xp_ref, ids_ref, gates_ref):
    x = h_ref[...]
    ms = jnp.mean(x * x, axis=-1, keepdims=True)
    n2 = (x * lax.rsqrt(ms + NORM_EPS)) * w_ref[...]
    hi = n2.astype(BF16)
    hi_f = hi.astype(F32)
    half = n2.shape[1] // 2
    xp_ref[...] = _pack_cols(hi_f[:, :half], hi_f[:, half:])

    lo = (n2 - hi_f).astype(BF16)
    wr = wr_ref[...]
    wr_hi = wr.astype(BF16)
    wr_lo = (wr - wr_hi.astype(F32)).astype(BF16)
    n_exp = wr.shape[1]
    tm = n2.shape[0]
    parts = jnp.dot(jnp.concatenate([hi, lo], axis=0), jnp.concatenate([wr_hi, wr_lo], axis=1),
                    preferred_element_type=F32)
    logits = ((parts[:tm, :n_exp] + parts[tm:, n_exp:])
              + (parts[:tm, n_exp:] + parts[tm:, :n_exp])) + br_ref[...]

    eid = lax.broadcasted_iota(jnp.int32, logits.shape, 1).astype(F32)
    neg = jnp.full_like(logits, -jnp.inf)
    g = logits
    vals, idxs = [], []
    for _ in range(TOP_K):
        mx = jnp.max(g, axis=-1, keepdims=True)
        first = jnp.min(jnp.where(g == mx, eid, float(n_exp)), axis=-1, keepdims=True)
        vals.append(mx)
        idxs.append(first)
        g = jnp.where(eid == first, neg, g)
    exps = [jnp.exp(v - vals[0]) for v in vals]
    denom = exps[0] + exps[1] + exps[2] + exps[3]
    ids_ref[...] = _lanes_from_columns(idxs, jnp.int32)
    gates_ref[...] = _lanes_from_columns([e / denom for e in exps], F32)


def _router(h, norm_w, w_router, b_router, tm):
    t, d = h.shape
    n_exp = w_router.shape[1]
    return pl.pallas_call(
        _router_kernel,
        out_shape=(jax.ShapeDtypeStruct((t, d // 2), jnp.uint32),
                   jax.ShapeDtypeStruct((t, TOP_K), jnp.int32),
                   jax.ShapeDtypeStruct((t, TOP_K), F32)),
        grid=(t // tm,),
        in_specs=[pl.BlockSpec((tm, d), lambda i: (i, 0)),
                  pl.BlockSpec((1, d), lambda i: (0, 0)),
                  pl.BlockSpec((d, n_exp), lambda i: (0, 0)),
                  pl.BlockSpec((1, n_exp), lambda i: (0, 0))],
        out_specs=(pl.BlockSpec((tm, d // 2), lambda i: (i, 0)),
                   pl.BlockSpec((tm, TOP_K), lambda i: (i, 0)),
                   pl.BlockSpec((tm, TOP_K), lambda i: (i, 0))),
        compiler_params=_params("parallel"),
        name="rmsnorm2_router",
    )(h, norm_w.reshape(1, d), w_router, b_router.reshape(1, n_exp))


def _rank_kernel(ids_ref, rank_ref, counts_ref, carry_ref, *, n_exp):
    i = pl.program_id(0)

    @pl.when(i == 0)
    def _():
        carry_ref[...] = jnp.zeros_like(carry_ref)

    ids = ids_ref[...]
    tm = ids.shape[0]
    eid = lax.broadcasted_iota(jnp.int32, (tm, n_exp), 1)
    onehots = [(eid == ids[:, r:r + 1]).astype(F32) for r in range(TOP_K)]
    total = onehots[0] + onehots[1] + onehots[2] + onehots[3]
    row = lax.broadcasted_iota(jnp.int32, (tm, tm), 0)
    col = lax.broadcasted_iota(jnp.int32, (tm, tm), 1)
    strict_lower = (col < row).astype(BF16)
    before = jnp.dot(strict_lower, total.astype(BF16), preferred_element_type=F32)
    before = before + carry_ref[...]
    ranks = [jnp.sum(oh * before, axis=-1, keepdims=True) for oh in onehots]
    rank_ref[...] = _lanes_from_columns(ranks, jnp.int32)
    carry_ref[...] += jnp.sum(total, axis=0, keepdims=True)
    counts_ref[...] = carry_ref[...]


def _rank(ids, n_exp, tm):
    t = ids.shape[0]
    return pl.pallas_call(
        functools.partial(_rank_kernel, n_exp=n_exp),
        out_shape=(jax.ShapeDtypeStruct((t, TOP_K), jnp.int32),
                   jax.ShapeDtypeStruct((1, n_exp), F32)),
        grid=(t // tm,),
        in_specs=[pl.BlockSpec((tm, TOP_K), lambda i: (i, 0))],
        out_specs=(pl.BlockSpec((tm, TOP_K), lambda i: (i, 0)),
                   pl.BlockSpec((1, n_exp), lambda i: (0, 0))),
        scratch_shapes=[pltpu.VMEM((1, n_exp), F32)],
        compiler_params=_params("arbitrary"),
        name="expert_rank",
    )(ids)


def _dispatch_kernel(fill_start_ref, fill_n_ref, pos_ref, x_ref, buf_ref,
                     zeros_ref, sem, zsem, *, n_exp, tile_rows):
    i = pl.program_id(0)
    tm = x_ref.shape[0]
    zrows = zeros_ref.shape[0]
    sub = SUBLANES
    n_bits = (zrows // sub - 1).bit_length()

    def head_copy(e, r):
        a = fill_start_ref[e]
        n_head = jnp.minimum((-a) & (sub - 1), fill_n_ref[e])
        return r < n_head, pltpu.make_async_copy(
            zeros_ref.at[pl.ds(0, 1), :], buf_ref.at[pl.ds(a + r, 1), :], zsem)

    def body_copy(e, bit):
        a = fill_start_ref[e]
        n_head = jnp.minimum((-a) & (sub - 1), fill_n_ref[e])
        groups = (fill_n_ref[e] - n_head) // sub
        size = sub << bit
        dst = pl.multiple_of(a + n_head + sub * (groups & ((1 << bit) - 1)), sub)
        return (groups & (1 << bit)) != 0, pltpu.make_async_copy(
            zeros_ref.at[pl.ds(0, size), :], buf_ref.at[pl.ds(dst, size), :], zsem)

    def fill_copies():
        for e in range(n_exp):
            for r in range(sub - 1):
                yield head_copy(e, r)
            for bit in range(n_bits):
                yield body_copy(e, bit)

    def tail_copy(c):
        dst = pl.multiple_of(c * zrows, zrows)
        return pltpu.make_async_copy(zeros_ref, buf_ref.at[pl.ds(dst, zrows), :], zsem)

    @pl.when(i == 0)
    def _():
        zeros_ref[...] = jnp.zeros(zeros_ref.shape, zeros_ref.dtype)
        first_tail = fill_start_ref[n_exp] // zrows
        n_chunks = buf_ref.shape[0] // zrows

        def tail_start(c, carry):
            tail_copy(c).start()
            return carry

        def tail_wait(c, carry):
            tail_copy(c).wait()
            return carry

        for live, cp in fill_copies():
            pl.when(live)(cp.start)
        lax.fori_loop(first_tail, n_chunks, tail_start, 0)
        for live, cp in fill_copies():
            pl.when(live)(cp.wait)
        lax.fori_loop(first_tail, n_chunks, tail_wait, 0)

    def start(tok, carry):
        src = x_ref.at[pl.ds(tok, 1), :]
        for slot in range(TOP_K):
            dst = buf_ref.at[pl.ds(pos_ref[tok * TOP_K + slot], 1), :]
            pltpu.make_async_copy(src, dst, sem).start()
        return carry

    lax.fori_loop(0, tm, start, 0, unroll=DMA_ISSUE_UNROLL)
    for _ in range(TOP_K):
        pltpu.make_async_copy(x_ref, buf_ref.at[pl.ds(0, tm), :], sem).wait()


def _dispatch(fill_start, fill_n, pos_flat, xp, n_rows, tm, tile_rows):
    t, w = xp.shape
    n_exp = fill_n.shape[0]
    zrows = tile_rows
    grid_spec = pltpu.PrefetchScalarGridSpec(
        num_scalar_prefetch=2,
        grid=(t // tm,),
        in_specs=[pl.BlockSpec((tm * TOP_K,), lambda i, fs, fn: (i,), memory_space=pltpu.SMEM),
                  pl.BlockSpec((tm, w), lambda i, fs, fn: (i, 0))],
        out_specs=pl.BlockSpec(memory_space=pl.ANY),
        scratch_shapes=[pltpu.VMEM((zrows, w), xp.dtype),
                        pltpu.SemaphoreType.DMA(()),
                        pltpu.SemaphoreType.DMA(())],
    )
    return pl.pallas_call(
        functools.partial(_dispatch_kernel, n_exp=n_exp, tile_rows=tile_rows),
        out_shape=jax.ShapeDtypeStruct((n_rows, w), xp.dtype),
        grid_spec=grid_spec,
        compiler_params=_params("arbitrary"),
        name="dispatch",
    )(fill_start, fill_n, pos_flat, xp)


def _deinterleave_matrix(n):
    r = lax.broadcasted_iota(jnp.int32, (n, n), 0)
    c = lax.broadcasted_iota(jnp.int32, (n, n), 1)
    src = jnp.where(c < n // 2, 2 * c, 2 * (c - n // 2) + 1)
    return (r == src).astype(BF16)


TILE_EMPTY, TILE_HALF, TILE_FULL = 0, 1, 2


def _for_tile_fill(fill, o_ref, compute):
    tm = o_ref.shape[0]
    pl.when(fill == TILE_FULL)(functools.partial(compute, tm))
    pl.when(fill == TILE_HALF)(functools.partial(compute, tm // 2))

    @pl.when(fill == TILE_EMPTY)
    def _():
        o_ref[...] = jnp.zeros_like(o_ref)


def _stage_expert_weights(w_hbm, stage_ref, sem, tabs, convert):
    te_ref, tf_ref, ne_ref, nw_ref = tabs
    c = pl.program_id(0)
    i = pl.program_id(1)
    width = stage_ref.shape[1]

    def block_copy(e, chunk):
        col = pl.multiple_of(chunk * width, width)
        return pltpu.make_async_copy(w_hbm.at[e, :, pl.ds(col, width)], stage_ref, sem)

    @pl.when((c == 0) & (i == 0))
    def _():
        block_copy(te_ref[0], 0).start()

    @pl.when(tf_ref[i] == 1)
    def _():
        block_copy(te_ref[i], c).wait()
        convert()
        next_chunk = c + nw_ref[i]

        @pl.when(next_chunk < pl.num_programs(0))
        def _():
            block_copy(ne_ref[i], next_chunk).start()


def _gateup_kernel(te_ref, tv_ref, tf_ref, tr_ref, ne_ref, nw_ref, x_ref, w_hbm, bg_ref, bu_ref,
                   o_ref, stage_ref, wg_ref, wu_ref, sem):
    del tr_ref
    i = pl.program_id(1)
    grp = V7X_MXU_DIM
    half = grp // 2

    def convert():
        perm = _deinterleave_matrix(grp)
        for g in range(stage_ref.shape[1] // grp):
            w = stage_ref[:, g * grp:(g + 1) * grp].astype(BF16)
            wp = jnp.dot(w, perm, preferred_element_type=F32).astype(BF16)
            wg_ref[:, g * half:(g + 1) * half] = wp[:, :half]
            wu_ref[:, g * half:(g + 1) * half] = wp[:, half:]

    _stage_expert_weights(w_hbm, stage_ref, sem, (te_ref, tf_ref, ne_ref, nw_ref), convert)

    def compute(rows):
        xa, xb = _unpack_cols(x_ref[:rows, :])
        k_half = xa.shape[1]

        def proj(w_s, b_ref):
            return (jnp.dot(xa, w_s[:k_half, :], preferred_element_type=F32)
                    + jnp.dot(xb, w_s[k_half:, :], preferred_element_type=F32)
                    + b_ref[...])

        g = jnp.minimum(proj(wg_ref, bg_ref), SWIGLU_LIMIT)
        u = jnp.clip(proj(wu_ref, bu_ref), -SWIGLU_LIMIT, SWIGLU_LIMIT)
        sig = 1.0 / (1.0 + jnp.exp(-(g * SWIGLU_ALPHA)))
        o_ref[:rows, :] = ((u + 1.0) * (g * sig)).astype(o_ref.dtype)
        if rows < o_ref.shape[0]:
            o_ref[rows:, :] = jnp.zeros((o_ref.shape[0] - rows, o_ref.shape[1]), o_ref.dtype)

    _for_tile_fill(tv_ref[i], o_ref, compute)


def _gateup(tabs, xs, w_gu, bg, bu, tm, fc):
    rows, k_half = xs.shape
    n_exp, d, f2 = w_gu.shape
    f = f2 // 2
    n_tiles = rows // tm
    n_chunks = f // fc

    def b_map(c, i, te, tv, tf, tr, ne, nw):
        return (te[i], 0, c)

    grid_spec = pltpu.PrefetchScalarGridSpec(
        num_scalar_prefetch=6,
        grid=(n_chunks, n_tiles),
        in_specs=[pl.BlockSpec((tm, k_half), lambda c, i, te, tv, tf, tr, ne, nw: (tr[i], 0)),
                  pl.BlockSpec(memory_space=pl.ANY),
                  pl.BlockSpec((None, 1, fc), b_map),
                  pl.BlockSpec((None, 1, fc), b_map)],
        out_specs=pl.BlockSpec((tm, fc), lambda c, i, te, tv, tf, tr, ne, nw: (i, c)),
        scratch_shapes=[pltpu.VMEM((d, 2 * fc), F32),
                        pltpu.VMEM((d, fc), BF16), pltpu.VMEM((d, fc), BF16),
                        pltpu.SemaphoreType.DMA(())],
    )
    return pl.pallas_call(
        _gateup_kernel,
        out_shape=jax.ShapeDtypeStruct((rows, f), BF16),
        grid_spec=grid_spec,
        compiler_params=_params("arbitrary", "arbitrary"),
        name="expert_gate_up",
    )(*tabs, xs, w_gu, bg, bu)


def _down_kernel(te_ref, tv_ref, tf_ref, ne_ref, nw_ref, a_ref, w_hbm, b_ref, o_ref,
                 stage_ref, wb_ref, sem):
    i = pl.program_id(1)

    def convert():
        wb_ref[...] = stage_ref[...].astype(BF16)

    _stage_expert_weights(w_hbm, stage_ref, sem, (te_ref, tf_ref, ne_ref, nw_ref), convert)

    def compute(rows):
        o = jnp.dot(a_ref[:rows, :], wb_ref[...], preferred_element_type=F32) + b_ref[...]
        ob = o.astype(BF16).astype(F32)
        half = ob.shape[1] // 2
        o_ref[:rows, :] = _pack_cols(ob[:, :half], ob[:, half:])
        if rows < o_ref.shape[0]:
            o_ref[rows:, :] = jnp.zeros((o_ref.shape[0] - rows, o_ref.shape[1]), o_ref.dtype)

    _for_tile_fill(tv_ref[i], o_ref, compute)


def _down(tabs, act, wd, bd, tm, tn):
    rows, f = act.shape
    n_exp, _, d = wd.shape
    n_chunks = d // tn

    grid_spec = pltpu.PrefetchScalarGridSpec(
        num_scalar_prefetch=5,
        grid=(n_chunks, rows // tm),
        in_specs=[pl.BlockSpec((tm, f), lambda c, i, te, tv, tf, ne, nw: (i, 0)),
                  pl.BlockSpec(memory_space=pl.ANY),
                  pl.BlockSpec((None, 1, tn), lambda c, i, te, tv, tf, ne, nw: (te[i], 0, c))],
        out_specs=pl.BlockSpec((tm, tn // 2), lambda c, i, te, tv, tf, ne, nw: (i, c)),
        scratch_shapes=[pltpu.VMEM((f, tn), F32), pltpu.VMEM((f, tn), BF16),
                        pltpu.SemaphoreType.DMA(())],
    )
    return pl.pallas_call(
        _down_kernel,
        out_shape=jax.ShapeDtypeStruct((rows, d // 2), jnp.uint32),
        grid_spec=grid_spec,
        compiler_params=_params("arbitrary", "arbitrary"),
        name="expert_down",
    )(*tabs, act, wd, bd)


def _combine_kernel(pos_ref, pos_next_ref, gates_ref, h_ref, w_ref, y_ref, o_ref, rows_ref, sems,
                    *, packed_chunk):
    i = pl.program_id(0)
    tm = h_ref.shape[0]
    cur = i & 1

    def request(p_ref, buf):
        def start(tok, carry):
            for slot in range(TOP_K):
                src = y_ref.at[pl.ds(p_ref[tok * TOP_K + slot], 1), :]
                dst = rows_ref.at[buf, slot, pl.ds(tok, 1), :]
                pltpu.make_async_copy(src, dst, sems.at[buf]).start()
            return carry

        lax.fori_loop(0, tm, start, 0, unroll=DMA_ISSUE_UNROLL)

    pl.when(i == 0)(functools.partial(request, pos_ref, 0))
    pl.when(i + 1 < pl.num_programs(0))(functools.partial(request, pos_next_ref, 1 - cur))
    for r in range(TOP_K):
        pltpu.make_async_copy(y_ref.at[pl.ds(0, tm), :], rows_ref.at[cur, r], sems.at[cur]).wait()

    gates = gates_ref[...]
    lo, hi = _unpack_cols_f32(rows_ref[cur, 0])
    moe_lo = gates[:, 0:1] * lo
    moe_hi = gates[:, 0:1] * hi
    for r in range(1, TOP_K):
        lo, hi = _unpack_cols_f32(rows_ref[cur, r])
        moe_lo += gates[:, r:r + 1] * lo
        moe_hi += gates[:, r:r + 1] * hi
    pieces = []
    for c in range(moe_lo.shape[1] // packed_chunk):
        cols = slice(c * packed_chunk, (c + 1) * packed_chunk)
        pieces += [moe_lo[:, cols], moe_hi[:, cols]]
    x = h_ref[...] + jnp.concatenate(pieces, axis=1)
    ms = jnp.mean(x * x, axis=-1, keepdims=True)
    o_ref[...] = (x * lax.rsqrt(ms + NORM_EPS)) * w_ref[...]


def _combine(pos_flat, gates, h, norm_w, ys, tm, down_cols):
    t, d = h.shape
    last = t // tm - 1
    return pl.pallas_call(
        functools.partial(_combine_kernel, packed_chunk=down_cols // 2),
        out_shape=jax.ShapeDtypeStruct((t, d), F32),
        grid=(t // tm,),
        in_specs=[pl.BlockSpec((tm * TOP_K,), lambda i: (i,), memory_space=pltpu.SMEM),
                  pl.BlockSpec((tm * TOP_K,), lambda i: (jnp.minimum(i + 1, last),),
                               memory_space=pltpu.SMEM),
                  pl.BlockSpec((tm, TOP_K), lambda i: (i, 0)),
                  pl.BlockSpec((tm, d), lambda i: (i, 0)),
                  pl.BlockSpec((1, d), lambda i: (0, 0)),
                  pl.BlockSpec(memory_space=pl.ANY)],
        out_specs=pl.BlockSpec((tm, d), lambda i: (i, 0)),
        scratch_shapes=[pltpu.VMEM((2, TOP_K, tm, d // 2), jnp.uint32),
                        pltpu.SemaphoreType.DMA((2,))],
        compiler_params=_params("arbitrary"),
        name="combine_final_norm",
    )(pos_flat, pos_flat, gates, h, norm_w.reshape(1, d), ys)


def _tiles(t, s, d, f):
    return dict(
        norm_rows=min(256, t),
        mm_rows=min(1024, t),
        mm_cols=min(1024, d),
        in_proj_cols=min(1536, 3 * d),
        attn_tile=min(512, s),
        router_rows=min(256, t),
        rank_rows=min(512, t),
        dispatch_rows=min(256, t),
        expert_rows=min(512, t),
        ff_chunk=min(512, f),
        down_cols=min(2048, d),
        combine_rows=min(256, t),
    )


def _alibi_slopes(n):
    return jnp.exp2(-ALIBI_MAX_BIAS * jnp.arange(1, n + 1, dtype=F32) / n)


def _routing_tables(ids, rank, counts, tm, n_tiles):
    n_exp = counts.shape[0]
    i32 = jnp.int32
    padded = ((counts + tm - 1) // tm) * tm
    ends = jnp.cumsum(padded)
    starts = ends - padded
    onehot = ids[..., None] == jnp.arange(n_exp, dtype=i32)
    pos = rank + jnp.sum(jnp.where(onehot, starts, 0), axis=-1)
    tile_start = jnp.arange(n_tiles, dtype=i32) * tm
    tile_valid = (tile_start < ends[-1]).astype(i32)
    tile_expert = jnp.minimum(jnp.sum((tile_start[:, None] >= ends[None, :]).astype(i32), axis=1),
                              n_exp - 1)
    prev_expert = jnp.concatenate([jnp.full((1,), -1, i32), tile_expert[:-1]])
    tile_first = tile_valid * (tile_expert != prev_expert).astype(i32)
    of_expert = tile_expert[:, None] == jnp.arange(n_exp, dtype=i32)[None, :]
    rows_end = jnp.sum(jnp.where(of_expert, (starts + counts)[None, :], 0), axis=1)
    tile_rows = jnp.clip(rows_end - tile_start, 0, tm)
    tile_fill = tile_valid * jnp.where(tile_rows <= tm // 2, TILE_HALF, TILE_FULL).astype(i32)
    tile_idx = jnp.arange(n_tiles, dtype=i32)
    later_first = (tile_idx[None, :] > tile_idx[:, None]) & (tile_first[None, :] == 1)
    next_first = jnp.min(jnp.where(later_first, tile_idx[None, :], n_tiles), axis=1)
    next_wraps = (next_first == n_tiles).astype(i32)
    next_expert = jnp.sum(jnp.where(tile_idx[None, :] == (next_first % n_tiles)[:, None],
                                    tile_expert[None, :], 0), axis=1).astype(i32)
    tile_row = jnp.minimum(jnp.arange(n_tiles, dtype=i32), ends[-1] // tm - 1)
    fill_start = jnp.concatenate([starts + counts, ends[-1:]]).astype(i32)
    fill_n = (padded - counts).astype(i32)
    tiles = dict(expert=tile_expert, fill=tile_fill, first=tile_first, row=tile_row,
                 next_expert=next_expert, next_wraps=next_wraps)
    return pos.reshape(-1), tiles, fill_start, fill_n


def _layer(h, l, norm1_w, w_in, lam_q1, lam_k1, lam_q2, lam_k2, subln_w, w_out, norm2_w,
           w_router, b_router, w_gate_up, b_gate_up, w_down, b_down, out_norm_w):
    b, s, d = h.shape
    t = b * s
    n_exp = w_router.shape[-1]
    f = w_down.shape[-2]
    n_heads = d // 2 // HEAD_DIM
    width = n_heads * HEAD_DIM
    tl = _tiles(t, s, d, f)
    lambda_init = 0.8 - 0.6 * math.exp(-0.3 * l)
    x2 = h.reshape(t, d)

    n1 = _rmsnorm(x2, norm1_w[l], tl["norm_rows"])
    n_cols = w_in.shape[-1]
    col_scale = jnp.ones((n_cols,), F32)
    col_scale = col_scale.at[:width].set(LOG2_E * DIFF_QK_DIM ** -0.5)
    col_scale = col_scale.at[3 * width:4 * width].set(LOG2_E * HEAD_DIM ** -0.5)
    proj = _matmul(n1, w_in[l].astype(BF16), col_scale.reshape(1, n_cols), tl["mm_rows"],
                   tl["in_proj_cols"], BF16)
    proj3 = proj.reshape(b, s, proj.shape[1])
    slopes = _alibi_slopes(n_heads)
    lam4 = jnp.stack([lam_q1[l], lam_k1[l], lam_q2[l], lam_k2[l]]).astype(F32)
    blocks = width // HEAD_DIM
    y_a = _diff_attention(proj3, slopes, lam4, subln_w[l].reshape(1, HEAD_DIM).astype(F32),
                          n_heads, 0, blocks, 2 * blocks, tl["attn_tile"], lambda_init)
    y_b = _moba_attention(proj3, slopes, n_heads, 3 * blocks, 4 * blocks, 5 * blocks,
                          tl["attn_tile"])
    h1 = _outproj(y_a.reshape(t, width), y_b.reshape(t, width), w_out[l].astype(BF16), x2,
                  tl["mm_rows"], tl["mm_cols"])

    xp, ids, gates = _router(h1, norm2_w[l], w_router[l], b_router[l], tl["router_rows"])
    rank, counts = _rank(ids, n_exp, tl["rank_rows"])
    tm = tl["expert_rows"]
    n_tiles = (t * TOP_K) // tm + n_exp
    pos_flat, tiles, fill_start, fill_n = _routing_tables(
        ids, rank, counts.reshape(n_exp).astype(jnp.int32), tm, n_tiles)

    xs = _dispatch(fill_start, fill_n, pos_flat, xp, n_tiles * tm, tl["dispatch_rows"], tm)
    bgu = b_gate_up[l].reshape(n_exp, 1, f, 2)
    gateup_tabs = (tiles["expert"], tiles["fill"], tiles["first"], tiles["row"],
                   tiles["next_expert"], tiles["next_wraps"])
    act = _gateup(gateup_tabs, xs, w_gate_up[l], bgu[..., 0], bgu[..., 1], tm, tl["ff_chunk"])
    down_tabs = (tiles["expert"], tiles["fill"], tiles["first"],
                 tiles["next_expert"], tiles["next_wraps"])
    ys = _down(down_tabs, act, w_down[l], b_down[l].reshape(n_exp, 1, d), tm, tl["down_cols"])
    return _combine(pos_flat, gates, h1, out_norm_w, ys, tl["combine_rows"],
                    tl["down_cols"]).reshape(b, s, d)


def kernel(x, norm1_w, w_in, lam_q1, lam_k1, lam_q2, lam_k2, subln_w, w_out, norm2_w,
           w_router, b_router, w_gate_up, b_gate_up, w_down, b_down, final_norm_w):
    depth = w_in.shape[0]
    assert depth == 1, "the fused combine + final norm stage assumes a single layer"
    return _layer(x, 0, norm1_w, w_in, lam_q1, lam_k1, lam_q2, lam_k2, subln_w, w_out,
                  norm2_w, w_router, b_router, w_gate_up, b_gate_up, w_down, b_down,
                  final_norm_w)
```
